```python
import jax, jax.numpy as jnp
from jax import lax
import numpy as np

D_MODEL = 2048
BATCH = 1
SEQ = 16384
DEPTH = 4

GRID_W = 64
CTX_LEN = 256
HEAD_DIM = 128
GQA_HEADS = 8
GQA_KV_HEADS = 2
LRU_WIDTH = 512
LRU_BLOCKS = 4
LRU_BLOCK_W = LRU_WIDTH // LRU_BLOCKS
CONV_WIDTH = 4
LRU_C = 8.0
MLA_HEADS = 4
MLA_Q_RANK = 384
MLA_KV_RANK = 256
MLA_NOPE = 128
MLA_ROPE = 64
MLA_V = 128
D_FF = 5632
N_EXPERTS = 8
TOP_K = 2
D_EXPERT = 7168
N_DENSE = (DEPTH + 1) // 2
N_MOE = DEPTH // 2
Q_BLOCK = 128
ROPE_THETA = 10000.0
EPS = 1e-6
DEEPNORM_ALPHA = (2 * DEPTH) ** 0.25
DEEPNORM_BETA = (8 * DEPTH) ** -0.25
IN_SPLITS = (GQA_HEADS * HEAD_DIM, GQA_KV_HEADS * HEAD_DIM, GQA_KV_HEADS * HEAD_DIM,
             LRU_WIDTH, LRU_WIDTH, MLA_Q_RANK, MLA_KV_RANK, MLA_ROPE)
IN_OFFSETS = tuple(int(v) for v in np.cumsum(IN_SPLITS)[:-1])
D_IN = sum(IN_SPLITS)
D_MIX = GQA_HEADS * HEAD_DIM + LRU_WIDTH + MLA_HEADS * MLA_V

kernel_name = 'hybrid_diffusion_gqa_rglru_mla_moe'


def rms_norm(x, g):
    xf = x.astype(jnp.float32)
    y = xf * lax.rsqrt(jnp.mean(jnp.square(xf), axis=-1, keepdims=True) + EPS)
    return (y * g).astype(x.dtype)


def layer_norm(x, g, b):
    xf = x.astype(jnp.float32)
    mu = jnp.mean(xf, axis=-1, keepdims=True)
    var = jnp.mean(jnp.square(xf - mu), axis=-1, keepdims=True)
    return ((xf - mu) * lax.rsqrt(var + EPS) * g + b).astype(x.dtype)


def axial_rope(rows, rot_dim):
    quarter = rot_dim // 4
    inv_freq = ROPE_THETA ** (-jnp.arange(quarter, dtype=jnp.float32) / quarter)
    row = jnp.repeat(jnp.arange(rows, dtype=jnp.float32), GRID_W)
    col = jnp.tile(jnp.arange(GRID_W, dtype=jnp.float32), rows)
    ang = jnp.concatenate([row[:, None] * inv_freq, col[:, None] * inv_freq], axis=-1)
    return jnp.cos(ang), jnp.sin(ang)


def apply_rope(x, cos, sin):
    shape = (1, cos.shape[0]) + (1,) * (x.ndim - 3) + (cos.shape[1],)
    c = cos.reshape(shape).astype(x.dtype)
    s = sin.reshape(shape).astype(x.dtype)
    x1, x2 = jnp.split(x, 2, axis=-1)
    return jnp.concatenate([x1 * c - x2 * s, x2 * c + x1 * s], axis=-1)


def attend(q, k, v, scale):
    s = jnp.einsum('bqhgd,bkhd->bhgqk', q, k, preferred_element_type=jnp.float32) * scale
    p = jax.nn.softmax(s, axis=-1).astype(v.dtype)
    return jnp.einsum('bhgqk,bkhd->bqhgd', p, v)


def latent_attention(q, k, v, k_ctx, v_ctx, scale):
    b, n = q.shape[:2]
    k_all = jnp.concatenate([k, k_ctx], axis=1)
    v_all = jnp.concatenate([v, v_ctx], axis=1)
    qb = jnp.moveaxis(q.reshape((b, n // Q_BLOCK, Q_BLOCK) + q.shape[2:]), 1, 0)
    out = lax.map(lambda blk: attend(blk, k_all, v_all, scale), qb)
    out = jnp.moveaxis(out, 0, 1)
    return out.reshape((b, n, -1))


def gqa_mixer(q_l, k_l, v_l, q_c, k_c, v_c, q_gain, k_gain, cos, sin, need_ctx):
    group = GQA_HEADS // GQA_KV_HEADS

    def queries(q):
        b, n = q.shape[:2]
        return rms_norm(q.reshape(b, n, GQA_KV_HEADS, group, HEAD_DIM), q_gain)

    def keys_values(k, v):
        b, n = k.shape[:2]
        return (rms_norm(k.reshape(b, n, GQA_KV_HEADS, HEAD_DIM), k_gain),
                v.reshape(b, n, GQA_KV_HEADS, HEAD_DIM))

    scale = HEAD_DIM ** -0.5
    kl, vl = keys_values(k_l, v_l)
    kc, vc = keys_values(k_c, v_c)
    ql = apply_rope(queries(q_l), cos, sin)
    kl = apply_rope(kl, cos, sin)
    out_l = latent_attention(ql, kl, vl, kc, vc, scale)
    out_c = None
    if need_ctx:
        oc = attend(queries(q_c), kc, vc, scale)
        out_c = oc.reshape(oc.shape[:2] + (-1,))
    return out_l, out_c


def centred_conv(x, w, b):
    n = x.shape[1]
    left = CONV_WIDTH // 2
    xp = jnp.pad(x, ((0, 0), (left, CONV_WIDTH - 1 - left), (0, 0)))
    out = b
    for tap in range(CONV_WIDTH):
        out = out + xp[:, tap:tap + n] * w[tap]
    return out


def block_diag(x, w, b):
    xb = x.reshape(x.shape[:-1] + (LRU_BLOCKS, LRU_BLOCK_W))
    return jnp.einsum('blni,nio->blno', xb, w).reshape(x.shape) + b


def lru_coeffs(x, w_r, b_r, w_i, b_i, lam):
    r = jax.nn.sigmoid(block_diag(x, w_r, b_r).astype(jnp.float32))
    i = jax.nn.sigmoid(block_diag(x, w_i, b_i).astype(jnp.float32))
    log_a = -LRU_C * r * jax.nn.softplus(-lam.astype(jnp.float32))
    a = jnp.exp(log_a)
    u = jnp.sqrt(-jnp.expm1(2.0 * log_a)) * i * x.astype(jnp.float32)
    return a, u


def linear_scan(a, u, h0):
    def combine(e1, e2):
        a1, b1 = e1
        a2, b2 = e2
        return a1 * a2, a2 * b1 + b2
    a_cum, u_cum = lax.associative_scan(combine, (a, u), axis=1)
    return a_cum * h0[:, None, :] + u_cum


def lru_direction(xc_c, xc_l, w_r, b_r, w_i, b_i, lam, reverse):
    if reverse:
        xc_c, xc_l = xc_c[:, ::-1], xc_l[:, ::-1]
    a_c, u_c = lru_coeffs(xc_c, w_r, b_r, w_i, b_i, lam)
    h_c = linear_scan(a_c, u_c, jnp.zeros_like(u_c[:, 0]))
    a_l, u_l = lru_coeffs(xc_l, w_r, b_r, w_i, b_i, lam)
    h_l = linear_scan(a_l, u_l, h_c[:, -1])
    if reverse:
        h_c, h_l = h_c[:, ::-1], h_l[:, ::-1]
    return h_c, h_l


def rglru_mixer(x_l, g_l, x_c, g_c, conv_w, conv_b, w_r, b_r, w_i, b_i, lam, need_ctx):
    xc_l = centred_conv(x_l, conv_w, conv_b)
    xc_c = centred_conv(x_c, conv_w, conv_b)
    hf_c, hf_l = lru_direction(xc_c, xc_l, w_r[0], b_r[0], w_i[0], b_i[0], lam[0], False)
    hb_c, hb_l = lru_direction(xc_c, xc_l, w_r[1], b_r[1], w_i[1], b_i[1], lam[1], True)
    out_l = jax.nn.gelu(g_l) * (hf_l + hb_l).astype(x_l.dtype)
    out_c = jax.nn.gelu(g_c) * (hf_c + hb_c).astype(x_c.dtype) if need_ctx else None
    return out_l, out_c


def mla_mixer(cq_l, ckv_l, kr_l, cq_c, ckv_c, kr_c, q_gain, w_uq, kv_gain, w_ukv, cos, sin, need_ctx):
    def queries(cq, rope):
        b, n = cq.shape[:2]
        q = (rms_norm(cq, q_gain) @ w_uq).reshape(b, n, MLA_HEADS, 1, MLA_NOPE + MLA_ROPE)
        if rope:
            q = jnp.concatenate([q[..., :MLA_NOPE], apply_rope(q[..., MLA_NOPE:], cos, sin)], axis=-1)
        return q

    def keys_values(ckv, kr, rope):
        b, n = ckv.shape[:2]
        kv = (rms_norm(ckv, kv_gain) @ w_ukv).reshape(b, n, MLA_HEADS, MLA_NOPE + MLA_V)
        k_rope = kr[:, :, None, :]
        if rope:
            k_rope = apply_rope(k_rope, cos, sin)
        k = jnp.concatenate([kv[..., :MLA_NOPE],
                             jnp.broadcast_to(k_rope, (b, n, MLA_HEADS, MLA_ROPE))], axis=-1)
        return k, kv[..., MLA_NOPE:]

    scale = (MLA_NOPE + MLA_ROPE) ** -0.5
    kl, vl = keys_values(ckv_l, kr_l, True)
    kc, vc = keys_values(ckv_c, kr_c, False)
    out_l = latent_attention(queries(cq_l, True), kl, vl, kc, vc, scale)
    out_c = None
    if need_ctx:
        oc = attend(queries(cq_c, False), kc, vc, scale)
        out_c = oc.reshape(oc.shape[:2] + (-1,))
    return out_l, out_c


def token_mixer(h_l, h_c, p, gqa_cos, gqa_sin, mla_cos, mla_sin, need_ctx):
    gq_l, gk_l, gv_l, lx_l, lg_l, cq_l, ckv_l, kr_l = jnp.split(h_l @ p['w_in'], IN_OFFSETS, axis=-1)
    gq_c, gk_c, gv_c, lx_c, lg_c, cq_c, ckv_c, kr_c = jnp.split(h_c @ p['w_in'], IN_OFFSETS, axis=-1)
    a_l, a_c = gqa_mixer(gq_l, gk_l, gv_l, gq_c, gk_c, gv_c, p['gqa_q_gain'], p['gqa_k_gain'],
                         gqa_cos, gqa_sin, need_ctx)
    r_l, r_c = rglru_mixer(lx_l, lg_l, lx_c, lg_c, p['conv_w'], p['conv_b'], p['w_r'], p['b_r'],
                           p['w_i'], p['b_i'], p['lam'], need_ctx)
    m_l, m_c = mla_mixer(cq_l, ckv_l, kr_l, cq_c, ckv_c, kr_c, p['mla_q_gain'], p['w_uq'],
                         p['mla_kv_gain'], p['w_ukv'], mla_cos, mla_sin, need_ctx)
    out_l = jnp.concatenate([a_l, r_l, m_l], axis=-1) @ p['w_out']
    out_c = jnp.concatenate([a_c, r_c, m_c], axis=-1) @ p['w_out'] if need_ctx else None
    return out_l, out_c


def swiglu(x, w_gate, w_up, w_down):
    return (jax.nn.silu(x @ w_gate) * (x @ w_up)) @ w_down


def moe_swiglu(x, router, w_gate, w_up, w_down):
    logits = (x @ router).astype(jnp.float32)
    top_v, top_i = lax.top_k(logits, TOP_K)
    gates = jax.nn.softmax(top_v, axis=-1)
    combine = jnp.sum(jax.nn.one_hot(top_i, N_EXPERTS, dtype=jnp.float32) * gates[..., None], axis=-2)
    combine = combine.astype(x.dtype)
    out = jnp.zeros_like(x)
    for e in range(N_EXPERTS):
        out = out + combine[..., e:e + 1] * swiglu(x, w_gate[e], w_up[e], w_down[e])
    return out


def modulation(cond, ada_w, ada_b):
    m = jax.nn.silu(cond) @ ada_w + ada_b
    return m.reshape(-1, 1, 6, D_MODEL)


def setup_inputs(seed: int = 0) -> dict:
    key = jax.random.key(seed)
    ks = iter(jax.random.split(key, 30))
    f32 = jnp.float32

    def nrm(shape, std):
        return jax.random.normal(next(ks), shape, f32) * std

    def gain(shape):
        return 1.0 + nrm(shape, 0.02)

    u = jax.random.uniform(next(ks), (DEPTH, 2, LRU_WIDTH), f32, 0.9, 0.999)
    a = u ** (1.0 / LRU_C)
    lam = jnp.log(a) - jnp.log1p(-a)
    return {
        'x': nrm((BATCH, SEQ, D_MODEL), 1.0),
        'c': nrm((BATCH, D_MODEL), 1.0),
        'ctx': nrm((BATCH, CTX_LEN, D_MODEL), 1.0),
        'c_ctx': nrm((D_MODEL,), 1.0),
        'ada_w': nrm((DEPTH, D_MODEL, 6 * D_MODEL), 0.5 * D_MODEL ** -0.5),
        'ada_b': nrm((DEPTH, 6 * D_MODEL), 0.01),
        'ln_g': gain((DEPTH, 2, D_MODEL)),
        'ln_b': nrm((DEPTH, 2, D_MODEL), 0.02),
        'w_in': nrm((DEPTH, D_MODEL, D_IN), D_MODEL ** -0.5),
        'w_out': nrm((DEPTH, D_MIX, D_MODEL), DEEPNORM_BETA * D_MIX ** -0.5),
        'gqa_q_gain': gain((DEPTH, HEAD_DIM)),
        'gqa_k_gain': gain((DEPTH, HEAD_DIM)),
        'lru_conv_w': nrm((DEPTH, CONV_WIDTH, LRU_WIDTH), CONV_WIDTH ** -0.5),
        'lru_conv_b': nrm((DEPTH, LRU_WIDTH), 0.02),
        'lru_w_r': nrm((DEPTH, 2, LRU_BLOCKS, LRU_BLOCK_W, LRU_BLOCK_W), LRU_BLOCK_W ** -0.5),
        'lru_b_r': nrm((DEPTH, 2, LRU_WIDTH), 0.02),
        'lru_w_i': nrm((DEPTH, 2, LRU_BLOCKS, LRU_BLOCK_W, LRU_BLOCK_W), LRU_BLOCK_W ** -0.5),
        'lru_b_i': nrm((DEPTH, 2, LRU_WIDTH), 0.02),
        'lru_lambda': lam,
        'mla_q_gain': gain((DEPTH, MLA_Q_RANK)),
        'mla_w_uq': nrm((DEPTH, MLA_Q_RANK, MLA_HEADS * (MLA_NOPE + MLA_ROPE)), MLA_Q_RANK ** -0.5),
        'mla_kv_gain': gain((DEPTH, MLA_KV_RANK)),
        'mla_w_ukv': nrm((DEPTH, MLA_KV_RANK, MLA_HEADS * (MLA_NOPE + MLA_V)), MLA_KV_RANK ** -0.5),
        'ffn_w_gate': nrm((N_DENSE, D_MODEL, D_FF), D_MODEL ** -0.5),
        'ffn_w_up': nrm((N_DENSE, D_MODEL, D_FF), D_MODEL ** -0.5),
        'ffn_w_down': nrm((N_DENSE, D_FF, D_MODEL), DEEPNORM_BETA * D_FF ** -0.5),
        'moe_router': nrm((N_MOE, D_MODEL, N_EXPERTS), D_MODEL ** -0.5),
        'moe_w_gate': nrm((N_MOE, N_EXPERTS, D_MODEL, D_EXPERT), D_MODEL ** -0.5),
        'moe_w_up': nrm((N_MOE, N_EXPERTS, D_MODEL, D_EXPERT), D_MODEL ** -0.5),
        'moe_w_down': nrm((N_MOE, N_EXPERTS, D_EXPERT, D_MODEL), DEEPNORM_BETA * D_EXPERT ** -0.5),
    }


def reference(x, c, ctx, c_ctx, ada_w, ada_b, ln_g, ln_b, w_in, w_out, gqa_q_gain, gqa_k_gain,
              lru_conv_w, lru_conv_b, lru_w_r, lru_b_r, lru_w_i, lru_b_i, lru_lambda,
              mla_q_gain, mla_w_uq, mla_kv_gain, mla_w_ukv,
              ffn_w_gate, ffn_w_up, ffn_w_down, moe_router, moe_w_gate, moe_w_up, moe_w_down):
    rows = x.shape[1] // GRID_W
    gqa_cos, gqa_sin = axial_rope(rows, HEAD_DIM)
    mla_cos, mla_sin = axial_rope(rows, MLA_ROPE)
    n_ctx = ctx.shape[1]
    for l in range(DEPTH):
        need_ctx = l < DEPTH - 1
        m_l = modulation(c, ada_w[l], ada_b[l])
        m_c = modulation(c_ctx[None], ada_w[l], ada_b[l])
        h_l = x * (1.0 + m_l[:, :, 1]) + m_l[:, :, 0]
        h_c = ctx * (1.0 + m_c[:, :, 1]) + m_c[:, :, 0]
        p = dict(w_in=w_in[l], w_out=w_out[l], gqa_q_gain=gqa_q_gain[l], gqa_k_gain=gqa_k_gain[l],
                 conv_w=lru_conv_w[l], conv_b=lru_conv_b[l], w_r=lru_w_r[l], b_r=lru_b_r[l],
                 w_i=lru_w_i[l], b_i=lru_b_i[l], lam=lru_lambda[l], mla_q_gain=mla_q_gain[l],
                 w_uq=mla_w_uq[l], mla_kv_gain=mla_kv_gain[l], w_ukv=mla_w_ukv[l])
        mix_l, mix_c = token_mixer(h_l, h_c, p, gqa_cos, gqa_sin, mla_cos, mla_sin, need_ctx)
        x = layer_norm(DEEPNORM_ALPHA * x + m_l[:, :, 2] * mix_l, ln_g[l, 0], ln_b[l, 0])
        f_in = x * (1.0 + m_l[:, :, 4]) + m_l[:, :, 3]
        if need_ctx:
            ctx = layer_norm(DEEPNORM_ALPHA * ctx + m_c[:, :, 2] * mix_c, ln_g[l, 0], ln_b[l, 0])
            f_in = jnp.concatenate([ctx * (1.0 + m_c[:, :, 4]) + m_c[:, :, 3], f_in], axis=1)
        if l % 2 == 0:
            f = swiglu(f_in, ffn_w_gate[l // 2], ffn_w_up[l // 2], ffn_w_down[l // 2])
        else:
            f = moe_swiglu(f_in, moe_router[l // 2], moe_w_gate[l // 2], moe_w_up[l // 2], moe_w_down[l // 2])
        if need_ctx:
            ctx = layer_norm(DEEPNORM_ALPHA * ctx + m_c[:, :, 5] * f[:, :n_ctx], ln_g[l, 1], ln_b[l, 1])
            f = f[:, n_ctx:]
        x = layer_norm(DEEPNORM_ALPHA * x + m_l[:, :, 5] * f, ln_g[l, 1], ln_b[l, 1])
    return x
```

```python
import functools
import math

import jax
import jax.numpy as jnp
import numpy as np
from jax import lax
from jax.experimental import pallas as pl
from jax.experimental.pallas import tpu as pltpu

GRID_W = 64
HEAD_DIM = 128
GQA_HEADS = 8
GQA_KV_HEADS = 2
LRU_WIDTH = 512
LRU_BLOCKS = 4
LRU_BLOCK_W = LRU_WIDTH // LRU_BLOCKS
CONV_WIDTH = 4
LRU_C = 8.0
MLA_HEADS = 4
MLA_Q_RANK = 384
MLA_KV_RANK = 256
MLA_NOPE = 128
MLA_ROPE = 64
MLA_V = 128
N_EXPERTS = 8
ROPE_THETA = 10000.0
EPS = 1e-6
LOG2E = 1.4426950408889634

LANES = 128
MLA_QK_PAD = 2 * LANES
OFF_GQ = 0
OFF_GK = OFF_GQ + GQA_HEADS * HEAD_DIM
OFF_GV = OFF_GK + GQA_KV_HEADS * HEAD_DIM
OFF_LX = OFF_GV + GQA_KV_HEADS * HEAD_DIM
OFF_LG = OFF_LX + LRU_WIDTH
OFF_CQ = OFF_LG + LRU_WIDTH
OFF_CKV = OFF_CQ + MLA_Q_RANK
OFF_KR = OFF_CKV + MLA_KV_RANK
D_IN = OFF_KR + MLA_ROPE
D_IN_PAD = OFF_KR + LANES
D_GQA = GQA_HEADS * HEAD_DIM
D_MLA = MLA_HEADS * MLA_V

ROW_TILE = 256
ATTN_TQ = 512
ATTN_TK = 512
FFN_TM = 640
MOE_TM = 512
FFN_TF = 512
GATHER_CHUNK = 256
VMEM_LIMIT = 56 * 1024 * 1024

BF16 = jnp.bfloat16
F32 = jnp.float32


def _params(sem, vmem=VMEM_LIMIT):
    return pltpu.CompilerParams(dimension_semantics=sem, vmem_limit_bytes=vmem)


def _mod_kernel(cond_ref, w_ref, b_ref, o_ref):
    cc = cond_ref[...]
    s = cc * jax.nn.sigmoid(cc)
    w = w_ref[0]
    m0 = jnp.sum(w * s[:, 0:1], axis=0, keepdims=True)
    m1 = jnp.sum(w * s[:, 1:2], axis=0, keepdims=True)
    o_ref[0] = jnp.concatenate([m0, m1], axis=0) + b_ref[0]


def _modulation(cond_t, ada_w, ada_b):
    depth, d, n = ada_w.shape
    tn = 1024 if n % 1024 == 0 else n
    return pl.pallas_call(
        _mod_kernel,
        grid=(depth, n // tn),
        in_specs=[pl.BlockSpec((d, 2), lambda l, j: (0, 0)),
                  pl.BlockSpec((1, d, tn), lambda l, j: (l, 0, j)),
                  pl.BlockSpec((1, 1, tn), lambda l, j: (l, 0, j))],
        out_specs=pl.BlockSpec((1, 2, tn), lambda l, j: (l, 0, j)),
        out_shape=jax.ShapeDtypeStruct((depth, 2, n), F32),
        compiler_params=_params(("arbitrary", "arbitrary")),
        name="modulation",
    )(cond_t, ada_w, ada_b.reshape(depth, 1, n))


def _rms(x, g):
    return x * lax.rsqrt(jnp.mean(x * x, axis=-1, keepdims=True) + EPS) * g


def _inproj_kernel(x_ref, mod_ref, w_ref, qg_ref, kg_ref, mqg_ref, wuq_ref, mkg_ref, wukv_ref,
                   cg_ref, sg_ref, cm_ref, sm_ref,
                   qg_o, kg_o, vg_o, qm_o, km_o, vm_o, lx_o, lg_o):
    shift = mod_ref[0, 0:1, :]
    scale = mod_ref[0, 1:2, :]
    h = x_ref[...] * (1.0 + scale) + shift
    proj = jnp.dot(h.astype(BF16), w_ref[...], preferred_element_type=F32)

    cg = cg_ref[...]
    sg = sg_ref[...]

    def rope_gqa(v):
        return v * cg + pltpu.roll(v, HEAD_DIM // 2, 1) * sg

    q_scale = HEAD_DIM ** -0.5 * LOG2E
    for hh in range(GQA_HEADS):
        q = _rms(proj[:, OFF_GQ + hh * HEAD_DIM: OFF_GQ + (hh + 1) * HEAD_DIM], qg_ref[...])
        qg_o[hh] = (rope_gqa(q) * q_scale).astype(BF16)
    for hh in range(GQA_KV_HEADS):
        k = _rms(proj[:, OFF_GK + hh * HEAD_DIM: OFF_GK + (hh + 1) * HEAD_DIM], kg_ref[...])
        kg_o[hh] = rope_gqa(k).astype(BF16)
        vg_o[hh] = proj[:, OFF_GV + hh * HEAD_DIM: OFF_GV + (hh + 1) * HEAD_DIM].astype(BF16)

    lx_o[...] = proj[:, OFF_LX:OFF_LX + LRU_WIDTH]
    lg_o[...] = proj[:, OFF_LG:OFF_LG + LRU_WIDTH]

    cq = _rms(proj[:, OFF_CQ:OFF_CQ + MLA_Q_RANK], mqg_ref[...])
    qm = jnp.dot(cq.astype(BF16), wuq_ref[...], preferred_element_type=F32)
    ckv = _rms(proj[:, OFF_CKV:OFF_CKV + MLA_KV_RANK], mkg_ref[...])
    kv = jnp.dot(ckv.astype(BF16), wukv_ref[...], preferred_element_type=F32)

    cm = cm_ref[...]
    sm = sm_ref[...]

    def rope_mla(v):
        swapped = pltpu.roll(v, MLA_ROPE // 2, 1) + pltpu.roll(v, LANES - MLA_ROPE // 2, 1)
        return v * cm + swapped * sm

    kr = rope_mla(proj[:, OFF_KR:OFF_KR + LANES])
    m_scale = (MLA_NOPE + MLA_ROPE) ** -0.5 * LOG2E
    for hh in range(MLA_HEADS):
        base = hh * MLA_QK_PAD
        qn = qm[:, base:base + MLA_NOPE]
        qr = rope_mla(qm[:, base + MLA_NOPE:base + MLA_QK_PAD])
        qm_o[hh] = (jnp.concatenate([qn, qr], axis=1) * m_scale).astype(BF16)
        kvb = hh * (MLA_NOPE + MLA_V)
        km_o[hh] = jnp.concatenate([kv[:, kvb:kvb + MLA_NOPE], kr], axis=1).astype(BF16)
        vm_o[hh] = kv[:, kvb + MLA_NOPE:kvb + MLA_NOPE + MLA_V].astype(BF16)


def _inproj(x, mod, w_in, q_gain, k_gain, mla_q_gain, w_uq, mla_kv_gain, w_ukv, tabs, n_lat):
    t, d = x.shape
    tm = ROW_TILE
    n_lat_tiles = n_lat // tm
    full = lambda a: pl.BlockSpec(a.shape, lambda i: (0,) * a.ndim)
    row = lambda w: pl.BlockSpec((tm, w), lambda i: (i, 0))
    heads = lambda n, w: pl.BlockSpec((n, tm, w), lambda i: (0, i, 0))
    cg, sg, cm, sm = tabs
    return pl.pallas_call(
        _inproj_kernel,
        grid=(t // tm,),
        in_specs=[row(d),
                  pl.BlockSpec((1, 6, d), lambda i: (jnp.where(i < n_lat_tiles, 0, 1), 0, 0)),
                  full(w_in), full(q_gain), full(k_gain), full(mla_q_gain), full(w_uq), full(mla_kv_gain),
                  full(w_ukv), row(LANES), row(LANES), row(LANES), row(LANES)],
        out_specs=[heads(GQA_HEADS, HEAD_DIM), heads(GQA_KV_HEADS, HEAD_DIM), heads(GQA_KV_HEADS, HEAD_DIM),
                   heads(MLA_HEADS, MLA_QK_PAD), heads(MLA_HEADS, MLA_QK_PAD), heads(MLA_HEADS, MLA_V),
                   row(LRU_WIDTH), row(LRU_WIDTH)],
        out_shape=[jax.ShapeDtypeStruct((GQA_HEADS, t, HEAD_DIM), BF16),
                   jax.ShapeDtypeStruct((GQA_KV_HEADS, t, HEAD_DIM), BF16),
                   jax.ShapeDtypeStruct((GQA_KV_HEADS, t, HEAD_DIM), BF16),
                   jax.ShapeDtypeStruct((MLA_HEADS, t, MLA_QK_PAD), BF16),
                   jax.ShapeDtypeStruct((MLA_HEADS, t, MLA_QK_PAD), BF16),
                   jax.ShapeDtypeStruct((MLA_HEADS, t, MLA_V), BF16),
                   jax.ShapeDtypeStruct((t, LRU_WIDTH), F32),
                   jax.ShapeDtypeStruct((t, LRU_WIDTH), F32)],
        compiler_params=_params(("arbitrary",)),
        name="inproj",
    )(x, mod, w_in, q_gain, k_gain, mla_q_gain, w_uq, mla_kv_gain, w_ukv, cg, sg, cm, sm)


def _flash_kernel(n_full, tk, tail, q_ref, k_ref, v_ref, o_ref):
    q = q_ref[0]
    tq = q.shape[0]
    dv = v_ref.shape[2]

    def chunk(start, size, carry):
        m, l, acc = carry
        k = k_ref[0, pl.ds(start, size), :]
        v = v_ref[0, pl.ds(start, size), :]
        s = lax.dot_general(q, k, (((1,), (1,)), ((), ())), preferred_element_type=F32)
        m_new = jnp.maximum(m, jnp.max(s, axis=-1, keepdims=True))
        alpha = jnp.exp2(m - m_new)
        p = jnp.exp2(s - m_new)
        l = alpha * l + jnp.sum(p, axis=-1, keepdims=True)
        acc = alpha * acc + jnp.dot(p.astype(BF16), v, preferred_element_type=F32)
        return m_new, l, acc

    carry = (jnp.full((tq, 1), -1e30, F32), jnp.zeros((tq, 1), F32), jnp.zeros((tq, dv), F32))
    if n_full:
        carry = lax.fori_loop(0, n_full, lambda c, car: chunk(pl.multiple_of(c * tk, tk), tk, car), carry)
    if tail:
        carry = chunk(n_full * tk, tail, carry)
    _, l, acc = carry
    o_ref[...] = (acc / l).astype(o_ref.dtype)


def _attention(q, k, v, n_lat, group, name):
    n_heads, t, dk = q.shape
    dv = v.shape[2]
    n_ctx = t - n_lat
    tq = ATTN_TQ if n_lat % ATTN_TQ == 0 else ROW_TILE
    tk = ATTN_TK if n_lat % ATTN_TK == 0 else ROW_TILE
    lat = pl.pallas_call(
        functools.partial(_flash_kernel, n_lat // tk, tk, n_ctx),
        grid=(n_heads, n_lat // tq),
        in_specs=[pl.BlockSpec((1, tq, dk), lambda h, i: (h, i, 0)),
                  pl.BlockSpec((1, t, dk), lambda h, i: (h // group, 0, 0)),
                  pl.BlockSpec((1, t, dv), lambda h, i: (h // group, 0, 0))],
        out_specs=pl.BlockSpec((tq, dv), lambda h, i: (i, h)),
        out_shape=jax.ShapeDtypeStruct((n_lat, n_heads * dv), BF16),
        compiler_params=_params(("arbitrary", "arbitrary")),
        name=name + "_latent",
    )(q, k, v)
    cblk = n_lat // n_ctx
    ctx = pl.pallas_call(
        functools.partial(_flash_kernel, 1, n_ctx, 0),
        grid=(n_heads,),
        in_specs=[pl.BlockSpec((1, n_ctx, dk), lambda h: (h, cblk, 0)),
                  pl.BlockSpec((1, n_ctx, dk), lambda h: (h // group, cblk, 0)),
                  pl.BlockSpec((1, n_ctx, dv), lambda h: (h // group, cblk, 0))],
        out_specs=pl.BlockSpec((n_ctx, dv), lambda h: (0, h)),
        out_shape=jax.ShapeDtypeStruct((n_ctx, n_heads * dv), BF16),
        compiler_params=_params(("arbitrary",)),
        name=name + "_context",
    )(q, k, v)
    return lat, ctx


def _conv_kernel(n_lat, t_total, xp_ref, xc_ref, xn_ref, w_ref, b_ref, o_ref):
    i = pl.program_id(0)
    r = xc_ref.shape[0]
    halo = 8
    ext = jnp.concatenate([xp_ref[r - halo:r, :], xc_ref[...], xn_ref[0:halo, :]], axis=0)
    rows = i * r + lax.broadcasted_iota(jnp.int32, (r, 1), 0)
    in_ctx = rows >= n_lat
    seg_lo = jnp.where(in_ctx, n_lat, 0)
    seg_hi = jnp.where(in_ctx, t_total, n_lat)
    left = CONV_WIDTH // 2
    out = jnp.broadcast_to(b_ref[...], (r, xc_ref.shape[1]))
    for tap in range(CONV_WIDTH):
        off = tap - left
        shifted = ext if off == 0 else pltpu.roll(ext, (-off) % (r + 2 * halo), 0)
        xs = shifted[halo:halo + r, :]
        src = rows + off
        valid = jnp.logical_and(src >= seg_lo, src < seg_hi)
        out = out + jnp.where(valid, xs, 0.0) * w_ref[tap:tap + 1, :]
    o_ref[...] = out


def _conv(lx, conv_w, conv_b, n_lat):
    t, w = lx.shape
    r = ROW_TILE
    nt = t // r
    return pl.pallas_call(
        functools.partial(_conv_kernel, n_lat, t),
        grid=(nt,),
        in_specs=[pl.BlockSpec((r, w), lambda i: (jnp.maximum(i - 1, 0), 0)),
                  pl.BlockSpec((r, w), lambda i: (i, 0)),
                  pl.BlockSpec((r, w), lambda i: (jnp.minimum(i + 1, nt - 1), 0)),
                  pl.BlockSpec(conv_w.shape, lambda i: (0, 0)),
                  pl.BlockSpec((1, w), lambda i: (0, 0))],
        out_specs=pl.BlockSpec((r, w), lambda i: (i, 0)),
        out_shape=jax.ShapeDtypeStruct((t, w), F32),
        compiler_params=_params(("arbitrary",)),
        name="lru_conv",
    )(lx, lx, lx, conv_w, conv_b.reshape(1, w))


def _lru_coeffs(x, d, wr_ref, br_ref, wi_ref, bi_ref, lam_ref):
    xb = x.astype(BF16)
    r_parts, i_parts = [], []
    for n in range(LRU_BLOCKS):
        xs = xb[:, n * LRU_BLOCK_W:(n + 1) * LRU_BLOCK_W]
        r_parts.append(jnp.dot(xs, wr_ref[d, n], preferred_element_type=F32))
        i_parts.append(jnp.dot(xs, wi_ref[d, n], preferred_element_type=F32))
    r = jax.nn.sigmoid(jnp.concatenate(r_parts, axis=1) + br_ref[d:d + 1, :])
    ig = jax.nn.sigmoid(jnp.concatenate(i_parts, axis=1) + bi_ref[d:d + 1, :])
    neg_lam = -lam_ref[d:d + 1, :]
    softplus = jnp.maximum(neg_lam, 0.0) + jnp.log1p(jnp.exp(-jnp.abs(neg_lam)))
    log_a = -LRU_C * r * softplus
    a = jnp.exp(log_a)
    u = jnp.sqrt(-jnp.tanh(log_a) * (a * a + 1.0)) * ig * x
    return a, u


def _tile_scan(a, u, reverse):
    r = a.shape[0]
    rows = lax.broadcasted_iota(jnp.int32, (r, 1), 0)
    d = 1
    while d < r:
        if reverse:
            a_sh = pltpu.roll(a, r - d, 0)
            u_sh = pltpu.roll(u, r - d, 0)
            ok = rows < r - d
        else:
            a_sh = pltpu.roll(a, d, 0)
            u_sh = pltpu.roll(u, d, 0)
            ok = rows >= d
        u = jnp.where(ok, a * u_sh + u, u)
        a = jnp.where(ok, a * a_sh, a)
        d *= 2
    return a, u


def _scan_kernel(nc_tiles, xf_ref, xb_ref, wr_ref, br_ref, wi_ref, bi_ref, lam_ref, hf_ref, hb_ref, carry_ref):
    s = pl.program_id(0)
    r = xf_ref.shape[0]

    @pl.when(s == 0)
    def _():
        carry_ref[...] = jnp.zeros_like(carry_ref)

    del nc_tiles
    a, u = _lru_coeffs(xf_ref[...], 0, wr_ref, br_ref, wi_ref, bi_ref, lam_ref)
    a, u = _tile_scan(a, u, False)
    h = a * carry_ref[0:1, :] + u
    hf_ref[...] = h
    carry_ref[0:1, :] = h[r - 1:r, :]

    a, u = _lru_coeffs(xb_ref[...], 1, wr_ref, br_ref, wi_ref, bi_ref, lam_ref)
    a, u = _tile_scan(a, u, True)
    h = a * carry_ref[1:2, :] + u
    hb_ref[...] = h
    carry_ref[1:2, :] = h[0:1, :]


def _lru_scan(xc, w_r, b_r, w_i, b_i, lam, n_lat):
    t, w = xc.shape
    r = ROW_TILE
    nt, nl = t // r, n_lat // r
    nc = nt - nl
    fwd = lambda s: (jnp.where(s < nc, nl + s, s - nc), 0)
    bwd = lambda s: (jnp.where(s < nc, nt - 1 - s, nl - 1 - (s - nc)), 0)
    full = lambda a: pl.BlockSpec(a.shape, lambda s: (0,) * a.ndim)
    return pl.pallas_call(
        functools.partial(_scan_kernel, nc),
        grid=(nt,),
        in_specs=[pl.BlockSpec((r, w), fwd), pl.BlockSpec((r, w), bwd),
                  full(w_r), full(b_r), full(w_i), full(b_i), full(lam)],
        out_specs=[pl.BlockSpec((r, w), fwd), pl.BlockSpec((r, w), bwd)],
        out_shape=[jax.ShapeDtypeStruct((t, w), F32), jax.ShapeDtypeStruct((t, w), F32)],
        scratch_shapes=[pltpu.VMEM((8, w), F32)],
        compiler_params=_params(("arbitrary",)),
        name="lru_scan",
    )(xc, xc, w_r, b_r, w_i, b_i, lam)


def _layer_norm(y, g, b):
    mu = jnp.mean(y, axis=-1, keepdims=True)
    yc = y - mu
    var = jnp.mean(yc * yc, axis=-1, keepdims=True)
    return yc * lax.rsqrt(var + EPS) * g + b


def _gelu_tanh(x):
    return 0.5 * x * (1.0 + jnp.tanh(math.sqrt(2.0 / math.pi) * (x + 0.044715 * (x * x * x))))


def _outproj_kernel(alpha, with_router, n_lat_tiles, al_ref, ac_ref, ml_ref, mc_ref, hf_ref, hb_ref, lg_ref, x_ref,
                    w_ref, mod_ref, g_ref, b_ref, *rest):
    if with_router:
        router_ref, x1_o, fin_o, topi_o, gate_o = rest
    else:
        x1_o, fin_o = rest
    is_lat = pl.program_id(0) < n_lat_tiles
    a = jnp.where(is_lat, al_ref[...], ac_ref[...])
    m = jnp.where(is_lat, ml_ref[...], mc_ref[...])
    rec = _gelu_tanh(lg_ref[...]) * (hf_ref[...] + hb_ref[...])
    mix = jnp.dot(a, w_ref[0:D_GQA, :], preferred_element_type=F32)
    mix += jnp.dot(rec.astype(BF16), w_ref[D_GQA:D_GQA + LRU_WIDTH, :], preferred_element_type=F32)
    mix += jnp.dot(m, w_ref[D_GQA + LRU_WIDTH:, :], preferred_element_type=F32)
    y = alpha * x_ref[...] + mod_ref[0, 2:3, :] * mix
    x1 = _layer_norm(y, g_ref[...], b_ref[...])
    x1_o[...] = x1
    f_in = x1 * (1.0 + mod_ref[0, 4:5, :]) + mod_ref[0, 3:4, :]
    fin_o[...] = f_in
    if with_router:
        logits = jnp.dot(f_in, router_ref[...], preferred_element_type=F32, precision=lax.Precision.HIGHEST)
        lane = lax.broadcasted_iota(jnp.int32, logits.shape, 1).astype(F32)
        neg = jnp.float32(-jnp.inf)
        lgt = jnp.where(lane < N_EXPERTS, logits, neg)
        v1 = jnp.max(lgt, axis=-1, keepdims=True)
        i1 = jnp.min(jnp.where(lgt == v1, lane, float(LANES)), axis=-1, keepdims=True)
        lgt2 = jnp.where(lane == i1, neg, lgt)
        v2 = jnp.max(lgt2, axis=-1, keepdims=True)
        i2 = jnp.min(jnp.where(lgt2 == v2, lane, float(LANES)), axis=-1, keepdims=True)
        e2 = jnp.exp(v2 - v1)
        den = 1.0 + e2
        topi_o[...] = jnp.where(lane == 0.0, i1, jnp.where(lane == 1.0, i2, 0.0)).astype(jnp.int32)
        gate_o[...] = jnp.where(lane == 0.0, 1.0 / den, jnp.where(lane == 1.0, e2 / den, 0.0))


def _outproj(a, m, hf, hb, lg, x, w_out, mod, ln_g, ln_b, alpha, n_lat, router):
    t, d = x.shape
    tm = ROW_TILE
    n_lat_tiles = n_lat // tm
    row = lambda w: pl.BlockSpec((tm, w), lambda i: (i, 0))
    lat_row = lambda w: pl.BlockSpec((tm, w), lambda i: (jnp.minimum(i, n_lat_tiles - 1), 0))
    ctx_row = lambda w: pl.BlockSpec((tm, w), lambda i: (jnp.maximum(i - n_lat_tiles, 0), 0))
    full = lambda arr: pl.BlockSpec(arr.shape, lambda i: (0,) * arr.ndim)
    in_specs = [lat_row(D_GQA), ctx_row(D_GQA), lat_row(D_MLA), ctx_row(D_MLA),
                row(LRU_WIDTH), row(LRU_WIDTH), row(LRU_WIDTH), row(d),
                full(w_out), pl.BlockSpec((1, 6, d), lambda i: (jnp.where(i < n_lat_tiles, 0, 1), 0, 0)),
                full(ln_g), full(ln_b)]
    out_specs = [row(d), row(d)]
    out_shape = [jax.ShapeDtypeStruct((t, d), F32), jax.ShapeDtypeStruct((t, d), F32)]
    args = [a[0], a[1], m[0], m[1], hf, hb, lg, x, w_out, mod, ln_g, ln_b]
    if router is not None:
        in_specs.append(full(router))
        args.append(router)
        out_specs += [row(LANES), row(LANES)]
        out_shape += [jax.ShapeDtypeStruct((t, LANES), jnp.int32), jax.ShapeDtypeStruct((t, LANES), F32)]
    return pl.pallas_call(
        functools.partial(_outproj_kernel, alpha, router is not None, n_lat_tiles),
        grid=(t // tm,),
        in_specs=in_specs, out_specs=out_specs, out_shape=out_shape,
        compiler_params=_params(("arbitrary",)),
        name="outproj",
    )(*args)


def _ffn_kernel(te_ref, nv_ref, x_ref, wg_ref, wu_ref, wd_ref, sc_ref, o_ref, acc_ref):
    del te_ref
    i, j = pl.program_id(0), pl.program_id(1)

    @pl.when(j == 0)
    def _():
        acc_ref[...] = jnp.zeros_like(acc_ref)

    @pl.when(i < nv_ref[0])
    def _():
        xb = x_ref[...].astype(BF16)
        g = jnp.dot(xb, wg_ref[0], preferred_element_type=F32)
        u = jnp.dot(xb, wu_ref[0], preferred_element_type=F32)
        hidden = (g * jax.nn.sigmoid(g)) * u
        acc_ref[...] += jnp.dot(hidden.astype(BF16), wd_ref[0], preferred_element_type=F32)

    @pl.when(j == pl.num_programs(1) - 1)
    def _():
        o_ref[...] = acc_ref[...] * sc_ref[...]


def _ffn(x, tile_expert, n_valid, w_gate, w_up, w_down, row_scale, tm):
    m_rows, d = x.shape
    f = w_gate.shape[2]
    tf = FFN_TF if f % FFN_TF == 0 else f
    grid_spec = pltpu.PrefetchScalarGridSpec(
        num_scalar_prefetch=2,
        grid=(m_rows // tm, f // tf),
        in_specs=[pl.BlockSpec((tm, d), lambda i, j, te, nv: (i, 0)),
                  pl.BlockSpec((1, d, tf), lambda i, j, te, nv: (te[i], 0, j)),
                  pl.BlockSpec((1, d, tf), lambda i, j, te, nv: (te[i], 0, j)),
                  pl.BlockSpec((1, tf, d), lambda i, j, te, nv: (te[i], j, 0)),
                  pl.BlockSpec((tm, 1), lambda i, j, te, nv: (i, 0))],
        out_specs=pl.BlockSpec((tm, d), lambda i, j, te, nv: (i, 0)),
        scratch_shapes=[pltpu.VMEM((tm, d), F32)],
    )
    return pl.pallas_call(
        _ffn_kernel,
        grid_spec=grid_spec,
        out_shape=jax.ShapeDtypeStruct((m_rows, d), F32),
        compiler_params=_params(("arbitrary", "arbitrary")),
        name="ffn",
    )(tile_expert, n_valid, x, w_gate, w_up, w_down, row_scale)


def _gather_kernel(idx_ref, src_ref, dst_ref, sem):
    base = pl.program_id(0) * GATHER_CHUNK

    def issue(r, carry):
        pltpu.make_async_copy(src_ref.at[pl.ds(idx_ref[base + r], 1)], dst_ref.at[pl.ds(base + r, 1)], sem).start()
        return carry

    lax.fori_loop(0, GATHER_CHUNK, issue, 0)
    pltpu.make_async_copy(src_ref.at[pl.ds(0, GATHER_CHUNK)], dst_ref.at[pl.ds(base, GATHER_CHUNK)], sem).wait()


def _row_gather(src, idx):
    m_rows = idx.shape[0]
    assert m_rows % GATHER_CHUNK == 0
    grid_spec = pltpu.PrefetchScalarGridSpec(
        num_scalar_prefetch=1,
        grid=(m_rows // GATHER_CHUNK,),
        in_specs=[pl.BlockSpec(memory_space=pl.ANY)],
        out_specs=pl.BlockSpec(memory_space=pl.ANY),
        scratch_shapes=[pltpu.SemaphoreType.DMA(())],
    )
    return pl.pallas_call(
        _gather_kernel,
        grid_spec=grid_spec,
        out_shape=jax.ShapeDtypeStruct((m_rows, src.shape[1]), src.dtype),
        compiler_params=_params(("arbitrary",)),
        name="row_gather",
    )(idx, src)


def _ln2_kernel(alpha, n_parts, x_ref, f_ref, mod_ref, g_ref, b_ref, o_ref):
    f = f_ref[0]
    for p in range(1, n_parts):
        f = f + f_ref[p]
    y = alpha * x_ref[...] + mod_ref[0, 5:6, :] * f
    o_ref[...] = _layer_norm(y, g_ref[...], b_ref[...])


def _ln2(x1, f_parts, mod, ln_g, ln_b, alpha, n_lat, n_rows):
    n_parts, _, d = f_parts.shape
    tm = ROW_TILE
    n_lat_tiles = n_lat // tm
    full = lambda arr: pl.BlockSpec(arr.shape, lambda i: (0,) * arr.ndim)
    return pl.pallas_call(
        functools.partial(_ln2_kernel, alpha, n_parts),
        grid=(n_rows // tm,),
        in_specs=[pl.BlockSpec((tm, d), lambda i: (i, 0)),
                  pl.BlockSpec((n_parts, tm, d), lambda i: (0, i, 0)),
                  pl.BlockSpec((1, 6, d), lambda i: (jnp.where(i < n_lat_tiles, 0, 1), 0, 0)),
                  full(ln_g), full(ln_b)],
        out_specs=pl.BlockSpec((tm, d), lambda i: (i, 0)),
        out_shape=jax.ShapeDtypeStruct((n_rows, d), F32),
        compiler_params=_params(("arbitrary",)),
        name="ln2",
    )(x1, f_parts, mod, ln_g, ln_b)


def _rope_tables(n_lat, n_ctx):
    rows = n_lat // GRID_W

    def angles(rot_dim):
        quarter = rot_dim // 4
        inv_freq = ROPE_THETA ** (-jnp.arange(quarter, dtype=F32) / quarter)
        row = jnp.repeat(jnp.arange(rows, dtype=F32), GRID_W)
        col = jnp.tile(jnp.arange(GRID_W, dtype=F32), rows)
        ang = jnp.concatenate([row[:, None] * inv_freq, col[:, None] * inv_freq], axis=-1)
        return jnp.cos(ang), jnp.sin(ang)

    def with_ctx(tab, fill):
        return jnp.concatenate([tab, jnp.full((n_ctx, tab.shape[1]), fill, F32)], axis=0)

    cos_g, sin_g = angles(HEAD_DIM)
    cg = with_ctx(jnp.concatenate([cos_g, cos_g], axis=1), 1.0)
    sg = with_ctx(jnp.concatenate([-sin_g, sin_g], axis=1), 0.0)
    cos_m, sin_m = angles(MLA_ROPE)
    zeros = jnp.zeros((n_lat, LANES - MLA_ROPE), F32)
    cm = with_ctx(jnp.concatenate([cos_m, cos_m, zeros], axis=1), 1.0)
    sm = with_ctx(jnp.concatenate([-sin_m, sin_m, zeros], axis=1), 0.0)
    return cg, sg, cm, sm


def _pad_w_uq(w_uq):
    depth, rank, _ = w_uq.shape
    w = w_uq.reshape(depth, rank, MLA_HEADS, MLA_NOPE + MLA_ROPE)
    w = jnp.pad(w, ((0, 0), (0, 0), (0, 0), (0, MLA_QK_PAD - MLA_NOPE - MLA_ROPE)))
    return w.reshape(depth, rank, MLA_HEADS * MLA_QK_PAD)


def _dispatch(top_i, gates, tm):
    t = top_i.shape[0]
    flat_e = top_i.T.reshape(-1)
    flat_g = gates.T.reshape(-1)
    onehot = (flat_e[:, None] == jnp.arange(N_EXPERTS, dtype=jnp.int32)[None, :]).astype(jnp.int32)
    rank = jnp.sum((jnp.cumsum(onehot, axis=0) - onehot) * onehot, axis=1)
    counts = jnp.sum(onehot, axis=0)
    ptiles = (counts + tm - 1) // tm
    tile_end = jnp.cumsum(ptiles)
    tile_start = tile_end - ptiles
    pos = (tile_start[flat_e] * tm + rank).astype(jnp.int32)
    n_tiles = (2 * t) // tm + N_EXPERTS
    m_pad = n_tiles * tm
    src = jnp.zeros((m_pad,), jnp.int32).at[pos].set(jnp.tile(jnp.arange(t, dtype=jnp.int32), 2))
    scale = jnp.zeros((m_pad,), F32).at[pos].set(flat_g)
    tile_expert = jnp.minimum(
        jnp.searchsorted(tile_end, jnp.arange(n_tiles, dtype=jnp.int32), side="right"), N_EXPERTS - 1
    ).astype(jnp.int32)
    n_valid = tile_end[-1:].astype(jnp.int32)
    return src, scale.reshape(m_pad, 1), pos, tile_expert, n_valid


def kernel(x, c, ctx, c_ctx, ada_w, ada_b, ln_g, ln_b, w_in, w_out, gqa_q_gain, gqa_k_gain, lru_conv_w, lru_conv_b, lru_w_r, lru_b_r, lru_w_i, lru_b_i, lru_lambda, mla_q_gain, mla_w_uq, mla_kv_gain, mla_w_ukv, ffn_w_gate, ffn_w_up, ffn_w_down, moe_router, moe_w_gate, moe_w_up, moe_w_down):
    batch, n_lat, d = x.shape
    n_ctx = ctx.shape[1]
    depth = ada_w.shape[0]
    assert batch == 1 and ctx.shape[0] == 1
    assert n_lat % ROW_TILE == 0 and n_ctx % ROW_TILE == 0 and n_lat % n_ctx == 0 and n_lat % GRID_W == 0
    t = n_lat + n_ctx
    alpha = (2 * depth) ** 0.25

    xs = jnp.concatenate([x[0], ctx[0]], axis=0)
    cond_t = jnp.stack([c[0], c_ctx], axis=1)
    mods = _modulation(cond_t, ada_w, ada_b).reshape(depth, 2, 6, d)
    tabs = _rope_tables(n_lat, n_ctx)

    w_in_b = jnp.pad(w_in, ((0, 0), (0, 0), (0, D_IN_PAD - D_IN))).astype(BF16)
    w_out_b = w_out.astype(BF16)
    w_uq_b = _pad_w_uq(mla_w_uq).astype(BF16)
    w_ukv_b = mla_w_ukv.astype(BF16)
    w_r_b = lru_w_r.astype(BF16)
    w_i_b = lru_w_i.astype(BF16)
    router_p = jnp.pad(moe_router, ((0, 0), (0, 0), (0, LANES - N_EXPERTS)))
    ffn_tm = FFN_TM if t % FFN_TM == 0 else ROW_TILE
    moe_tm = MOE_TM if (2 * t) % MOE_TM == 0 else ROW_TILE

    for l in range(depth):
        last = l == depth - 1
        mod = mods[l]
        qg, kg, vg, qm, km, vm, lx, lg = _inproj(
            xs, mod, w_in_b[l], gqa_q_gain[l][None], gqa_k_gain[l][None], mla_q_gain[l][None], w_uq_b[l],
            mla_kv_gain[l][None], w_ukv_b[l], tabs, n_lat)
        a_out = _attention(qg, kg, vg, n_lat, GQA_HEADS // GQA_KV_HEADS, "gqa")
        m_out = _attention(qm, km, vm, n_lat, 1, "mla")
        xc = _conv(lx, lru_conv_w[l], lru_conv_b[l], n_lat)
        hf, hb = _lru_scan(xc, w_r_b[l], lru_b_r[l], w_i_b[l], lru_b_i[l], lru_lambda[l], n_lat)
        moe = l % 2 == 1
        outs = _outproj(a_out, m_out, hf, hb, lg, xs, w_out_b[l], mod, ln_g[l, 0][None], ln_b[l, 0][None],
                        alpha, n_lat, router_p[l // 2] if moe else None)
        if moe:
            x1, f_in, topi, gate = outs
            src, scale, pos, tile_expert, n_valid = _dispatch(topi[:, :2], gate[:, :2], moe_tm)
            x_sorted = _row_gather(f_in, src)
            y = _ffn(x_sorted, tile_expert, n_valid, moe_w_gate[l // 2].astype(BF16),
                     moe_w_up[l // 2].astype(BF16), moe_w_down[l // 2].astype(BF16), scale, moe_tm)
            f_parts = _row_gather(y, pos).reshape(2, t, d)
        else:
            x1, f_in = outs
            n_tiles = t // ffn_tm
            y = _ffn(f_in, jnp.zeros((n_tiles,), jnp.int32), jnp.full((1,), n_tiles, jnp.int32),
                     ffn_w_gate[l // 2][None].astype(BF16), ffn_w_up[l // 2][None].astype(BF16),
                     ffn_w_down[l // 2][None].astype(BF16), jnp.ones((t, 1), F32), ffn_tm)
            f_parts = y[None]
        xs = _ln2(x1, f_parts, mod, ln_g[l, 1][None], ln_b[l, 1][None], alpha, n_lat, n_lat if last else t)
    return xs[None]
```

```python
import functools
import math

import jax
import jax.numpy as jnp
import numpy as np
from jax import lax
from jax.experimental import pallas as pl
from jax.experimental.pallas import tpu as pltpu

GRID_W = 64
HEAD_DIM = 128
GQA_HEADS = 8
GQA_KV_HEADS = 2
LRU_WIDTH = 512
LRU_BLOCKS = 4
LRU_BLOCK_W = LRU_WIDTH // LRU_BLOCKS
CONV_WIDTH = 4
LRU_C = 8.0
MLA_HEADS = 4
MLA_Q_RANK = 384
MLA_KV_RANK = 256
MLA_NOPE = 128
MLA_ROPE = 64
MLA_V = 128
N_EXPERTS = 8
ROPE_THETA = 10000.0
EPS = 1e-6
LOG2E = 1.4426950408889634

LANES = 128
MLA_QK_PAD = 2 * LANES
OFF_GQ = 0
OFF_GK = OFF_GQ + GQA_HEADS * HEAD_DIM
OFF_GV = OFF_GK + GQA_KV_HEADS * HEAD_DIM
OFF_LX = OFF_GV + GQA_KV_HEADS * HEAD_DIM
OFF_LG = OFF_LX + LRU_WIDTH
OFF_CQ = OFF_LG + LRU_WIDTH
OFF_CKV = OFF_CQ + MLA_Q_RANK
OFF_KR = OFF_CKV + MLA_KV_RANK
D_IN = OFF_KR + MLA_ROPE
D_IN_PAD = OFF_KR + LANES
D_GQA = GQA_HEADS * HEAD_DIM
D_MLA = MLA_HEADS * MLA_V

ROW_TILE = 256
ATTN_TQ = 512
ATTN_TK = 512
FFN_TM = 640
MOE_TM = 512
FFN_TF = 512
GATHER_CHUNK = 256
VMEM_LIMIT = 56 * 1024 * 1024

BF16 = jnp.bfloat16
F32 = jnp.float32


def _params(sem, vmem=VMEM_LIMIT):
    return pltpu.CompilerParams(dimension_semantics=sem, vmem_limit_bytes=vmem)


def _mod_kernel(cond_ref, w_ref, b_ref, o_ref):
    cc = cond_ref[...]
    s = cc * jax.nn.sigmoid(cc)
    w = w_ref[0]
    m0 = jnp.sum(w * s[:, 0:1], axis=0, keepdims=True)
    m1 = jnp.sum(w * s[:, 1:2], axis=0, keepdims=True)
    o_ref[0] = jnp.concatenate([m0, m1], axis=0) + b_ref[0]


def _modulation(cond_t, ada_w, ada_b):
    depth, d, n = ada_w.shape
    tn = 1024 if n % 1024 == 0 else n
    return pl.pallas_call(
        _mod_kernel,
        grid=(depth, n // tn),
        in_specs=[pl.BlockSpec((d, 2), lambda l, j: (0, 0)),
                  pl.BlockSpec((1, d, tn), lambda l, j: (l, 0, j)),
                  pl.BlockSpec((1, 1, tn), lambda l, j: (l, 0, j))],
        out_specs=pl.BlockSpec((1, 2, tn), lambda l, j: (l, 0, j)),
        out_shape=jax.ShapeDtypeStruct((depth, 2, n), F32),
        compiler_params=_params(("arbitrary", "arbitrary")),
        name="modulation",
    )(cond_t, ada_w, ada_b.reshape(depth, 1, n))


def _rms(x, g):
    return x * lax.rsqrt(jnp.mean(x * x, axis=-1, keepdims=True) + EPS) * g


def _inproj_kernel(x_ref, mod_ref, w_ref, qg_ref, kg_ref, mqg_ref, wuq_ref, mkg_ref, wukv_ref,
                   cg_ref, sg_ref, cm_ref, sm_ref,
                   qg_o, kg_o, vg_o, qm_o, km_o, vm_o, lx_o, lg_o):
    shift = mod_ref[0, 0:1, :]
    scale = mod_ref[0, 1:2, :]
    h = x_ref[...] * (1.0 + scale) + shift
    proj = jnp.dot(h.astype(BF16), w_ref[...], preferred_element_type=F32)

    cg = cg_ref[...]
    sg = sg_ref[...]

    def rope_gqa(v):
        return v * cg + pltpu.roll(v, HEAD_DIM // 2, 1) * sg

    q_scale = HEAD_DIM ** -0.5 * LOG2E
    for hh in range(GQA_HEADS):
        q = _rms(proj[:, OFF_GQ + hh * HEAD_DIM: OFF_GQ + (hh + 1) * HEAD_DIM], qg_ref[...])
        qg_o[hh] = (rope_gqa(q) * q_scale).astype(BF16)
    for hh in range(GQA_KV_HEADS):
        k = _rms(proj[:, OFF_GK + hh * HEAD_DIM: OFF_GK + (hh + 1) * HEAD_DIM], kg_ref[...])
        kg_o[hh] = rope_gqa(k).astype(BF16)
        vg_o[hh] = proj[:, OFF_GV + hh * HEAD_DIM: OFF_GV + (hh + 1) * HEAD_DIM].T.astype(BF16)

    lx_o[...] = proj[:, OFF_LX:OFF_LX + LRU_WIDTH]
    lg_o[...] = proj[:, OFF_LG:OFF_LG + LRU_WIDTH]

    cq = _rms(proj[:, OFF_CQ:OFF_CQ + MLA_Q_RANK], mqg_ref[...])
    qm = jnp.dot(cq.astype(BF16), wuq_ref[...], preferred_element_type=F32)
    ckv = _rms(proj[:, OFF_CKV:OFF_CKV + MLA_KV_RANK], mkg_ref[...])
    kv = jnp.dot(ckv.astype(BF16), wukv_ref[...], preferred_element_type=F32)

    cm = cm_ref[...]
    sm = sm_ref[...]

    def rope_mla(v):
        swapped = pltpu.roll(v, MLA_ROPE // 2, 1) + pltpu.roll(v, LANES - MLA_ROPE // 2, 1)
        return v * cm + swapped * sm

    kr = rope_mla(proj[:, OFF_KR:OFF_KR + LANES])
    m_scale = (MLA_NOPE + MLA_ROPE) ** -0.5 * LOG2E
    for hh in range(MLA_HEADS):
        base = hh * MLA_QK_PAD
        qn = qm[:, base:base + MLA_NOPE]
        qr = rope_mla(qm[:, base + MLA_NOPE:base + MLA_QK_PAD])
        qm_o[hh] = (jnp.concatenate([qn, qr], axis=1) * m_scale).astype(BF16)
        kvb = hh * (MLA_NOPE + MLA_V)
        km_o[hh] = jnp.concatenate([kv[:, kvb:kvb + MLA_NOPE], kr], axis=1).astype(BF16)
        vm_o[hh] = kv[:, kvb + MLA_NOPE:kvb + MLA_NOPE + MLA_V].T.astype(BF16)


def _inproj(x, mod, w_in, q_gain, k_gain, mla_q_gain, w_uq, mla_kv_gain, w_ukv, tabs, n_lat):
    t, d = x.shape
    tm = ROW_TILE
    n_lat_tiles = n_lat // tm
    full = lambda a: pl.BlockSpec(a.shape, lambda i: (0,) * a.ndim)
    row = lambda w: pl.BlockSpec((tm, w), lambda i: (i, 0))
    heads = lambda n, w: pl.BlockSpec((n, tm, w), lambda i: (0, i, 0))
    heads_t = lambda n, w: pl.BlockSpec((n, w, tm), lambda i: (0, 0, i))
    cg, sg, cm, sm = tabs
    return pl.pallas_call(
        _inproj_kernel,
        grid=(t // tm,),
        in_specs=[row(d),
                  pl.BlockSpec((1, 6, d), lambda i: (jnp.where(i < n_lat_tiles, 0, 1), 0, 0)),
                  full(w_in), full(q_gain), full(k_gain), full(mla_q_gain), full(w_uq), full(mla_kv_gain),
                  full(w_ukv), row(LANES), row(LANES), row(LANES), row(LANES)],
        out_specs=[heads(GQA_HEADS, HEAD_DIM), heads(GQA_KV_HEADS, HEAD_DIM), heads_t(GQA_KV_HEADS, HEAD_DIM),
                   heads(MLA_HEADS, MLA_QK_PAD), heads(MLA_HEADS, MLA_QK_PAD), heads_t(MLA_HEADS, MLA_V),
                   row(LRU_WIDTH), row(LRU_WIDTH)],
        out_shape=[jax.ShapeDtypeStruct((GQA_HEADS, t, HEAD_DIM), BF16),
                   jax.ShapeDtypeStruct((GQA_KV_HEADS, t, HEAD_DIM), BF16),
                   jax.ShapeDtypeStruct((GQA_KV_HEADS, HEAD_DIM, t), BF16),
                   jax.ShapeDtypeStruct((MLA_HEADS, t, MLA_QK_PAD), BF16),
                   jax.ShapeDtypeStruct((MLA_HEADS, t, MLA_QK_PAD), BF16),
                   jax.ShapeDtypeStruct((MLA_HEADS, MLA_V, t), BF16),
                   jax.ShapeDtypeStruct((t, LRU_WIDTH), F32),
                   jax.ShapeDtypeStruct((t, LRU_WIDTH), F32)],
        compiler_params=_params(("arbitrary",)),
        name="inproj",
    )(x, mod, w_in, q_gain, k_gain, mla_q_gain, w_uq, mla_kv_gain, w_ukv, cg, sg, cm, sm)


ONES_ROWS = 16


def _scores_t(q, k_ref, start, size):
    k = k_ref[0, pl.ds(start, size), :]
    return lax.dot_general(k, q, (((1,), (1,)), ((), ())), preferred_element_type=F32)


def _softmax_step(st, vt_ref, start, size, carry):
    m, acc = carry
    ones = jnp.where(lax.broadcasted_iota(jnp.int32, (ONES_ROWS, size), 0) == 0, 1.0, 0.0).astype(BF16)
    vt = jnp.concatenate([vt_ref[0, :, pl.ds(start, size)], ones], axis=0)
    m_new = jnp.maximum(m, jnp.max(st, axis=0, keepdims=True))
    alpha = jnp.exp2(m - m_new)
    pt = jnp.exp2(st - m_new).astype(BF16)
    acc = alpha * acc + jnp.dot(vt, pt, preferred_element_type=F32)
    return m_new, acc


def _attn_init(tq, dv):
    return jnp.full((1, tq), -1e30, F32), jnp.zeros((dv + ONES_ROWS, tq), F32)


def _attn_finish(carry, o_ref):
    _, acc = carry
    dv = acc.shape[0] - ONES_ROWS
    o_ref[...] = (acc[:dv, :] / acc[dv:dv + 1, :]).T.astype(o_ref.dtype)


def _attn_latent_kernel(n_full, tk, tail, q_ref, k_ref, vt_ref, o_ref, s_ref):
    q = q_ref[0]
    dv = vt_ref.shape[1]

    def body(jj, carry):
        c0 = pl.multiple_of(jj * (2 * tk), 2 * tk)
        s_ref[1] = _scores_t(q, k_ref, c0 + tk, tk)
        carry = _softmax_step(s_ref[0], vt_ref, c0, tk, carry)
        s_ref[0] = _scores_t(q, k_ref, c0 + 2 * tk, tk)
        return _softmax_step(s_ref[1], vt_ref, c0 + tk, tk, carry)

    carry = _attn_init(q.shape[0], dv)
    s_ref[0] = _scores_t(q, k_ref, 0, tk)
    n_pairs = n_full // 2
    carry = lax.fori_loop(0, n_pairs - 1, body, carry, unroll=2)
    c0 = (n_pairs - 1) * 2 * tk
    s_ref[1] = _scores_t(q, k_ref, c0 + tk, tk)
    carry = _softmax_step(s_ref[0], vt_ref, c0, tk, carry)
    s_tail = _scores_t(q, k_ref, n_full * tk, tail)
    carry = _softmax_step(s_ref[1], vt_ref, c0 + tk, tk, carry)
    carry = _softmax_step(s_tail, vt_ref, n_full * tk, tail, carry)
    _attn_finish(carry, o_ref)


def _attn_context_kernel(q_ref, k_ref, vt_ref, o_ref):
    n = k_ref.shape[1]
    carry = _attn_init(q_ref.shape[1], vt_ref.shape[1])
    carry = _softmax_step(_scores_t(q_ref[0], k_ref, 0, n), vt_ref, 0, n, carry)
    _attn_finish(carry, o_ref)


def _attention(q, k, vt, n_lat, group, name):
    n_heads, t, dk = q.shape
    dv = vt.shape[1]
    n_ctx = t - n_lat
    tq = ATTN_TQ if n_lat % ATTN_TQ == 0 else ROW_TILE
    tk = ATTN_TK if n_lat % (2 * ATTN_TK) == 0 else ROW_TILE
    assert n_lat % (2 * tk) == 0 and n_ctx > 0
    lat = pl.pallas_call(
        functools.partial(_attn_latent_kernel, n_lat // tk, tk, n_ctx),
        grid=(n_heads, n_lat // tq),
        in_specs=[pl.BlockSpec((1, tq, dk), lambda h, i: (h, i, 0)),
                  pl.BlockSpec((1, t, dk), lambda h, i: (h // group, 0, 0)),
                  pl.BlockSpec((1, dv, t), lambda h, i: (h // group, 0, 0))],
        out_specs=pl.BlockSpec((tq, dv), lambda h, i: (i, h)),
        out_shape=jax.ShapeDtypeStruct((n_lat, n_heads * dv), BF16),
        scratch_shapes=[pltpu.VMEM((2, tk, tq), F32)],
        compiler_params=_params(("arbitrary", "arbitrary")),
        name=name + "_latent",
    )(q, k, vt)
    cblk = n_lat // n_ctx
    ctx = pl.pallas_call(
        _attn_context_kernel,
        grid=(n_heads,),
        in_specs=[pl.BlockSpec((1, n_ctx, dk), lambda h: (h, cblk, 0)),
                  pl.BlockSpec((1, n_ctx, dk), lambda h: (h // group, cblk, 0)),
                  pl.BlockSpec((1, dv, n_ctx), lambda h: (h // group, 0, cblk))],
        out_specs=pl.BlockSpec((n_ctx, dv), lambda h: (0, h)),
        out_shape=jax.ShapeDtypeStruct((n_ctx, n_heads * dv), BF16),
        compiler_params=_params(("arbitrary",)),
        name=name + "_context",
    )(q, k, vt)
    return lat, ctx


def _conv_kernel(n_lat, t_total, xp_ref, xc_ref, xn_ref, w_ref, b_ref, o_ref):
    i = pl.program_id(0)
    r = xc_ref.shape[0]
    halo = 8
    ext = jnp.concatenate([xp_ref[r - halo:r, :], xc_ref[...], xn_ref[0:halo, :]], axis=0)
    rows = i * r + lax.broadcasted_iota(jnp.int32, (r, 1), 0)
    in_ctx = rows >= n_lat
    seg_lo = jnp.where(in_ctx, n_lat, 0)
    seg_hi = jnp.where(in_ctx, t_total, n_lat)
    left = CONV_WIDTH // 2
    out = jnp.broadcast_to(b_ref[...], (r, xc_ref.shape[1]))
    for tap in range(CONV_WIDTH):
        off = tap - left
        shifted = ext if off == 0 else pltpu.roll(ext, (-off) % (r + 2 * halo), 0)
        xs = shifted[halo:halo + r, :]
        src = rows + off
        valid = jnp.logical_and(src >= seg_lo, src < seg_hi)
        out = out + jnp.where(valid, xs, 0.0) * w_ref[tap:tap + 1, :]
    o_ref[...] = out


def _conv(lx, conv_w, conv_b, n_lat):
    t, w = lx.shape
    r = ROW_TILE
    nt = t // r
    return pl.pallas_call(
        functools.partial(_conv_kernel, n_lat, t),
        grid=(nt,),
        in_specs=[pl.BlockSpec((r, w), lambda i: (jnp.maximum(i - 1, 0), 0)),
                  pl.BlockSpec((r, w), lambda i: (i, 0)),
                  pl.BlockSpec((r, w), lambda i: (jnp.minimum(i + 1, nt - 1), 0)),
                  pl.BlockSpec(conv_w.shape, lambda i: (0, 0)),
                  pl.BlockSpec((1, w), lambda i: (0, 0))],
        out_specs=pl.BlockSpec((r, w), lambda i: (i, 0)),
        out_shape=jax.ShapeDtypeStruct((t, w), F32),
        compiler_params=_params(("arbitrary",)),
        name="lru_conv",
    )(lx, lx, lx, conv_w, conv_b.reshape(1, w))


def _lru_coeffs(x, d, wr_ref, br_ref, wi_ref, bi_ref, lam_ref):
    xb = x.astype(BF16)
    r_parts, i_parts = [], []
    for n in range(LRU_BLOCKS):
        xs = xb[:, n * LRU_BLOCK_W:(n + 1) * LRU_BLOCK_W]
        r_parts.append(jnp.dot(xs, wr_ref[d, n], preferred_element_type=F32))
        i_parts.append(jnp.dot(xs, wi_ref[d, n], preferred_element_type=F32))
    r = jax.nn.sigmoid(jnp.concatenate(r_parts, axis=1) + br_ref[d:d + 1, :])
    ig = jax.nn.sigmoid(jnp.concatenate(i_parts, axis=1) + bi_ref[d:d + 1, :])
    neg_lam = -lam_ref[d:d + 1, :]
    softplus = jnp.maximum(neg_lam, 0.0) + jnp.log1p(jnp.exp(-jnp.abs(neg_lam)))
    log_a = -LRU_C * r * softplus
    a = jnp.exp(log_a)
    u = jnp.sqrt(-jnp.tanh(log_a) * (a * a + 1.0)) * ig * x
    return a, u


def _tile_scan(a, u, reverse):
    r = a.shape[0]
    rows = lax.broadcasted_iota(jnp.int32, (r, 1), 0)
    d = 1
    while d < r:
        if reverse:
            a_sh = pltpu.roll(a, r - d, 0)
            u_sh = pltpu.roll(u, r - d, 0)
            ok = rows < r - d
        else:
            a_sh = pltpu.roll(a, d, 0)
            u_sh = pltpu.roll(u, d, 0)
            ok = rows >= d
        u = jnp.where(ok, a * u_sh + u, u)
        a = jnp.where(ok, a * a_sh, a)
        d *= 2
    return a, u


def _scan_kernel(nc_tiles, xf_ref, xb_ref, wr_ref, br_ref, wi_ref, bi_ref, lam_ref, hf_ref, hb_ref, carry_ref):
    s = pl.program_id(0)
    r = xf_ref.shape[0]

    @pl.when(s == 0)
    def _():
        carry_ref[...] = jnp.zeros_like(carry_ref)

    del nc_tiles
    a, u = _lru_coeffs(xf_ref[...], 0, wr_ref, br_ref, wi_ref, bi_ref, lam_ref)
    a, u = _tile_scan(a, u, False)
    h = a * carry_ref[0:1, :] + u
    hf_ref[...] = h
    carry_ref[0:1, :] = h[r - 1:r, :]

    a, u = _lru_coeffs(xb_ref[...], 1, wr_ref, br_ref, wi_ref, bi_ref, lam_ref)
    a, u = _tile_scan(a, u, True)
    h = a * carry_ref[1:2, :] + u
    hb_ref[...] = h
    carry_ref[1:2, :] = h[0:1, :]


def _lru_scan(xc, w_r, b_r, w_i, b_i, lam, n_lat):
    t, w = xc.shape
    r = ROW_TILE
    nt, nl = t // r, n_lat // r
    nc = nt - nl
    fwd = lambda s: (jnp.where(s < nc, nl + s, s - nc), 0)
    bwd = lambda s: (jnp.where(s < nc, nt - 1 - s, nl - 1 - (s - nc)), 0)
    full = lambda a: pl.BlockSpec(a.shape, lambda s: (0,) * a.ndim)
    return pl.pallas_call(
        functools.partial(_scan_kernel, nc),
        grid=(nt,),
        in_specs=[pl.BlockSpec((r, w), fwd), pl.BlockSpec((r, w), bwd),
                  full(w_r), full(b_r), full(w_i), full(b_i), full(lam)],
        out_specs=[pl.BlockSpec((r, w), fwd), pl.BlockSpec((r, w), bwd)],
        out_shape=[jax.ShapeDtypeStruct((t, w), F32), jax.ShapeDtypeStruct((t, w), F32)],
        scratch_shapes=[pltpu.VMEM((8, w), F32)],
        compiler_params=_params(("arbitrary",)),
        name="lru_scan",
    )(xc, xc, w_r, b_r, w_i, b_i, lam)


def _layer_norm(y, g, b):
    mu = jnp.mean(y, axis=-1, keepdims=True)
    yc = y - mu
    var = jnp.mean(yc * yc, axis=-1, keepdims=True)
    return yc * lax.rsqrt(var + EPS) * g + b


def _gelu_tanh(x):
    return 0.5 * x * (1.0 + jnp.tanh(math.sqrt(2.0 / math.pi) * (x + 0.044715 * (x * x * x))))


def _outproj_kernel(alpha, with_router, n_lat_tiles, al_ref, ac_ref, ml_ref, mc_ref, hf_ref, hb_ref, lg_ref, x_ref,
                    w_ref, mod_ref, g_ref, b_ref, *rest):
    if with_router:
        router_ref, x1_o, fin_o, topi_o, gate_o = rest
    else:
        x1_o, fin_o = rest
    is_lat = pl.program_id(0) < n_lat_tiles
    a = jnp.where(is_lat, al_ref[...], ac_ref[...])
    m = jnp.where(is_lat, ml_ref[...], mc_ref[...])
    rec = _gelu_tanh(lg_ref[...]) * (hf_ref[...] + hb_ref[...])
    mix = jnp.dot(a, w_ref[0:D_GQA, :], preferred_element_type=F32)
    mix += jnp.dot(rec.astype(BF16), w_ref[D_GQA:D_GQA + LRU_WIDTH, :], preferred_element_type=F32)
    mix += jnp.dot(m, w_ref[D_GQA + LRU_WIDTH:, :], preferred_element_type=F32)
    y = alpha * x_ref[...] + mod_ref[0, 2:3, :] * mix
    x1 = _layer_norm(y, g_ref[...], b_ref[...])
    x1_o[...] = x1
    f_in = x1 * (1.0 + mod_ref[0, 4:5, :]) + mod_ref[0, 3:4, :]
    fin_o[...] = f_in
    if with_router:
        f_hi = f_in.astype(BF16)
        f_lo = (f_in - f_hi.astype(F32)).astype(BF16)
        r = router_ref[...]
        r_hi = r.astype(BF16)
        r_lo = (r - r_hi.astype(F32)).astype(BF16)
        logits = (jnp.dot(f_hi, r_hi, preferred_element_type=F32) + jnp.dot(f_hi, r_lo, preferred_element_type=F32)
                  + jnp.dot(f_lo, r_hi, preferred_element_type=F32))
        lane = lax.broadcasted_iota(jnp.int32, logits.shape, 1).astype(F32)
        neg = jnp.float32(-jnp.inf)
        lgt = jnp.where(lane < N_EXPERTS, logits, neg)
        v1 = jnp.max(lgt, axis=-1, keepdims=True)
        i1 = jnp.min(jnp.where(lgt == v1, lane, float(LANES)), axis=-1, keepdims=True)
        lgt2 = jnp.where(lane == i1, neg, lgt)
        v2 = jnp.max(lgt2, axis=-1, keepdims=True)
        i2 = jnp.min(jnp.where(lgt2 == v2, lane, float(LANES)), axis=-1, keepdims=True)
        e2 = jnp.exp(v2 - v1)
        den = 1.0 + e2
        topi_o[...] = jnp.where(lane == 0.0, i1, jnp.where(lane == 1.0, i2, 0.0)).astype(jnp.int32)
        gate_o[...] = jnp.where(lane == 0.0, 1.0 / den, jnp.where(lane == 1.0, e2 / den, 0.0))


def _outproj(a, m, hf, hb, lg, x, w_out, mod, ln_g, ln_b, alpha, n_lat, router):
    t, d = x.shape
    tm = ROW_TILE
    n_lat_tiles = n_lat // tm
    row = lambda w: pl.BlockSpec((tm, w), lambda i: (i, 0))
    lat_row = lambda w: pl.BlockSpec((tm, w), lambda i: (jnp.minimum(i, n_lat_tiles - 1), 0))
    ctx_row = lambda w: pl.BlockSpec((tm, w), lambda i: (jnp.maximum(i - n_lat_tiles, 0), 0))
    full = lambda arr: pl.BlockSpec(arr.shape, lambda i: (0,) * arr.ndim)
    in_specs = [lat_row(D_GQA), ctx_row(D_GQA), lat_row(D_MLA), ctx_row(D_MLA),
                row(LRU_WIDTH), row(LRU_WIDTH), row(LRU_WIDTH), row(d),
                full(w_out), pl.BlockSpec((1, 6, d), lambda i: (jnp.where(i < n_lat_tiles, 0, 1), 0, 0)),
                full(ln_g), full(ln_b)]
    out_specs = [row(d), row(d)]
    out_shape = [jax.ShapeDtypeStruct((t, d), F32), jax.ShapeDtypeStruct((t, d), F32)]
    args = [a[0], a[1], m[0], m[1], hf, hb, lg, x, w_out, mod, ln_g, ln_b]
    if router is not None:
        in_specs.append(full(router))
        args.append(router)
        out_specs += [row(LANES), row(LANES)]
        out_shape += [jax.ShapeDtypeStruct((t, LANES), jnp.int32), jax.ShapeDtypeStruct((t, LANES), F32)]
    return pl.pallas_call(
        functools.partial(_outproj_kernel, alpha, router is not None, n_lat_tiles),
        grid=(t // tm,),
        in_specs=in_specs, out_specs=out_specs, out_shape=out_shape,
        compiler_params=_params(("arbitrary",)),
        name="outproj",
    )(*args)


def _ffn_kernel(te_ref, nv_ref, x_ref, wg_ref, wu_ref, wd_ref, sc_ref, o_ref, acc_ref):
    del te_ref
    i, j = pl.program_id(0), pl.program_id(1)

    @pl.when(j == 0)
    def _():
        acc_ref[...] = jnp.zeros_like(acc_ref)

    @pl.when(i < nv_ref[0])
    def _():
        xb = x_ref[...].astype(BF16)
        g = jnp.dot(xb, wg_ref[0], preferred_element_type=F32)
        u = jnp.dot(xb, wu_ref[0], preferred_element_type=F32)
        hidden = (g * jax.nn.sigmoid(g)) * u
        acc_ref[...] += jnp.dot(hidden.astype(BF16), wd_ref[0], preferred_element_type=F32)

    @pl.when(j == pl.num_programs(1) - 1)
    def _():
        o_ref[...] = acc_ref[...] * sc_ref[...]


def _ffn(x, tile_expert, n_valid, w_gate, w_up, w_down, row_scale, tm):
    m_rows, d = x.shape
    f = w_gate.shape[2]
    tf = FFN_TF if f % FFN_TF == 0 else f
    nj = f // tf
    col = lambda i, j, nv: jnp.where(i < nv[0], j, nj - 1)
    grid_spec = pltpu.PrefetchScalarGridSpec(
        num_scalar_prefetch=2,
        grid=(m_rows // tm, nj),
        in_specs=[pl.BlockSpec((tm, d), lambda i, j, te, nv: (i, 0)),
                  pl.BlockSpec((1, d, tf), lambda i, j, te, nv: (te[i], 0, col(i, j, nv))),
                  pl.BlockSpec((1, d, tf), lambda i, j, te, nv: (te[i], 0, col(i, j, nv))),
                  pl.BlockSpec((1, tf, d), lambda i, j, te, nv: (te[i], col(i, j, nv), 0)),
                  pl.BlockSpec((tm, 1), lambda i, j, te, nv: (i, 0))],
        out_specs=pl.BlockSpec((tm, d), lambda i, j, te, nv: (i, 0)),
        scratch_shapes=[pltpu.VMEM((tm, d), F32)],
    )
    return pl.pallas_call(
        _ffn_kernel,
        grid_spec=grid_spec,
        out_shape=jax.ShapeDtypeStruct((m_rows, d), F32),
        compiler_params=_params(("arbitrary", "arbitrary")),
        name="ffn",
    )(tile_expert, n_valid, x, w_gate, w_up, w_down, row_scale)


def _gather_kernel(idx_ref, src_ref, dst_ref, sem):
    base = pl.program_id(0) * GATHER_CHUNK

    def issue(r, carry):
        pltpu.make_async_copy(src_ref.at[pl.ds(idx_ref[base + r], 1)], dst_ref.at[pl.ds(r, 1)], sem).start()
        return carry

    lax.fori_loop(0, GATHER_CHUNK, issue, 0, unroll=8)
    pltpu.make_async_copy(src_ref.at[pl.ds(0, GATHER_CHUNK)], dst_ref, sem).wait()


def _row_gather(src, idx):
    m_rows = idx.shape[0]
    assert m_rows % GATHER_CHUNK == 0
    grid_spec = pltpu.PrefetchScalarGridSpec(
        num_scalar_prefetch=1,
        grid=(m_rows // GATHER_CHUNK,),
        in_specs=[pl.BlockSpec(memory_space=pl.ANY)],
        out_specs=pl.BlockSpec((GATHER_CHUNK, src.shape[1]), lambda c, idx: (c, 0)),
        scratch_shapes=[pltpu.SemaphoreType.DMA(())],
    )
    return pl.pallas_call(
        _gather_kernel,
        grid_spec=grid_spec,
        out_shape=jax.ShapeDtypeStruct((m_rows, src.shape[1]), src.dtype),
        compiler_params=_params(("arbitrary",)),
        name="row_gather",
    )(idx, src)


def _ln2_kernel(alpha, n_parts, x_ref, f_ref, mod_ref, g_ref, b_ref, o_ref):
    f = f_ref[0]
    for p in range(1, n_parts):
        f = f + f_ref[p]
    y = alpha * x_ref[...] + mod_ref[0, 5:6, :] * f
    o_ref[...] = _layer_norm(y, g_ref[...], b_ref[...])


def _ln2(x1, f_parts, mod, ln_g, ln_b, alpha, n_lat, n_rows):
    n_parts, _, d = f_parts.shape
    tm = ROW_TILE
    n_lat_tiles = n_lat // tm
    full = lambda arr: pl.BlockSpec(arr.shape, lambda i: (0,) * arr.ndim)
    return pl.pallas_call(
        functools.partial(_ln2_kernel, alpha, n_parts),
        grid=(n_rows // tm,),
        in_specs=[pl.BlockSpec((tm, d), lambda i: (i, 0)),
                  pl.BlockSpec((n_parts, tm, d), lambda i: (0, i, 0)),
                  pl.BlockSpec((1, 6, d), lambda i: (jnp.where(i < n_lat_tiles, 0, 1), 0, 0)),
                  full(ln_g), full(ln_b)],
        out_specs=pl.BlockSpec((tm, d), lambda i: (i, 0)),
        out_shape=jax.ShapeDtypeStruct((n_rows, d), F32),
        compiler_params=_params(("arbitrary",)),
        name="ln2",
    )(x1, f_parts, mod, ln_g, ln_b)


def _rope_tables(n_lat, n_ctx):
    rows = n_lat // GRID_W

    def angles(rot_dim):
        quarter = rot_dim // 4
        inv_freq = ROPE_THETA ** (-jnp.arange(quarter, dtype=F32) / quarter)
        row = jnp.repeat(jnp.arange(rows, dtype=F32), GRID_W)
        col = jnp.tile(jnp.arange(GRID_W, dtype=F32), rows)
        ang = jnp.concatenate([row[:, None] * inv_freq, col[:, None] * inv_freq], axis=-1)
        return jnp.cos(ang), jnp.sin(ang)

    def with_ctx(tab, fill):
        return jnp.concatenate([tab, jnp.full((n_ctx, tab.shape[1]), fill, F32)], axis=0)

    cos_g, sin_g = angles(HEAD_DIM)
    cg = with_ctx(jnp.concatenate([cos_g, cos_g], axis=1), 1.0)
    sg = with_ctx(jnp.concatenate([-sin_g, sin_g], axis=1), 0.0)
    cos_m, sin_m = angles(MLA_ROPE)
    zeros = jnp.zeros((n_lat, LANES - MLA_ROPE), F32)
    cm = with_ctx(jnp.concatenate([cos_m, cos_m, zeros], axis=1), 1.0)
    sm = with_ctx(jnp.concatenate([-sin_m, sin_m, zeros], axis=1), 0.0)
    return cg, sg, cm, sm


def _pad_w_uq(w_uq):
    depth, rank, _ = w_uq.shape
    w = w_uq.reshape(depth, rank, MLA_HEADS, MLA_NOPE + MLA_ROPE)
    w = jnp.pad(w, ((0, 0), (0, 0), (0, 0), (0, MLA_QK_PAD - MLA_NOPE - MLA_ROPE)))
    return w.reshape(depth, rank, MLA_HEADS * MLA_QK_PAD)


def _dispatch(top_i, gates, tm):
    t = top_i.shape[0]
    flat_e = top_i.T.reshape(-1)
    flat_g = gates.T.reshape(-1)
    onehot = (flat_e[:, None] == jnp.arange(N_EXPERTS, dtype=jnp.int32)[None, :]).astype(jnp.int32)
    rank = jnp.sum((jnp.cumsum(onehot, axis=0) - onehot) * onehot, axis=1)
    counts = jnp.sum(onehot, axis=0)
    ptiles = (counts + tm - 1) // tm
    tile_end = jnp.cumsum(ptiles)
    tile_start = tile_end - ptiles
    pos = (tile_start[flat_e] * tm + rank).astype(jnp.int32)
    n_tiles = (2 * t) // tm + N_EXPERTS
    m_pad = n_tiles * tm
    src = jnp.zeros((m_pad,), jnp.int32).at[pos].set(jnp.tile(jnp.arange(t, dtype=jnp.int32), 2))
    scale = jnp.zeros((m_pad,), F32).at[pos].set(flat_g)
    tile_ids = jnp.arange(n_tiles, dtype=jnp.int32)
    last_used = jnp.max(jnp.where(ptiles > 0, jnp.arange(N_EXPERTS, dtype=jnp.int32), 0))
    tile_expert = jnp.minimum(jnp.sum((tile_end[None, :] <= tile_ids[:, None]).astype(jnp.int32), axis=1),
                              last_used).astype(jnp.int32)
    n_valid = tile_end[-1:].astype(jnp.int32)
    return src, scale.reshape(m_pad, 1), pos, tile_expert, n_valid


def kernel(x, c, ctx, c_ctx, ada_w, ada_b, ln_g, ln_b, w_in, w_out, gqa_q_gain, gqa_k_gain, lru_conv_w, lru_conv_b, lru_w_r, lru_b_r, lru_w_i, lru_b_i, lru_lambda, mla_q_gain, mla_w_uq, mla_kv_gain, mla_w_ukv, ffn_w_gate, ffn_w_up, ffn_w_down, moe_router, moe_w_gate, moe_w_up, moe_w_down):
    batch, n_lat, d = x.shape
    n_ctx = ctx.shape[1]
    depth = ada_w.shape[0]
    assert batch == 1 and ctx.shape[0] == 1
    assert n_lat % ROW_TILE == 0 and n_ctx % ROW_TILE == 0 and n_lat % n_ctx == 0 and n_lat % GRID_W == 0
    t = n_lat + n_ctx
    alpha = (2 * depth) ** 0.25

    xs = jnp.concatenate([x[0], ctx[0]], axis=0)
    cond_t = jnp.stack([c[0], c_ctx], axis=1)
    mods = _modulation(cond_t, ada_w, ada_b).reshape(depth, 2, 6, d)
    tabs = _rope_tables(n_lat, n_ctx)

    w_in_b = jnp.pad(w_in, ((0, 0), (0, 0), (0, D_IN_PAD - D_IN))).astype(BF16)
    w_out_b = w_out.astype(BF16)
    w_uq_b = _pad_w_uq(mla_w_uq).astype(BF16)
    w_ukv_b = mla_w_ukv.astype(BF16)
    w_r_b = lru_w_r.astype(BF16)
    w_i_b = lru_w_i.astype(BF16)
    router_p = jnp.pad(moe_router, ((0, 0), (0, 0), (0, LANES - N_EXPERTS)))
    ffn_tm = FFN_TM if t % FFN_TM == 0 else ROW_TILE
    moe_tm = MOE_TM if (2 * t) % MOE_TM == 0 else ROW_TILE

    for l in range(depth):
        last = l == depth - 1
        mod = mods[l]
        qg, kg, vg, qm, km, vm, lx, lg = _inproj(
            xs, mod, w_in_b[l], gqa_q_gain[l][None], gqa_k_gain[l][None], mla_q_gain[l][None], w_uq_b[l],
            mla_kv_gain[l][None], w_ukv_b[l], tabs, n_lat)
        a_out = _attention(qg, kg, vg, n_lat, GQA_HEADS // GQA_KV_HEADS, "gqa")
        m_out = _attention(qm, km, vm, n_lat, 1, "mla")
        xc = _conv(lx, lru_conv_w[l], lru_conv_b[l], n_lat)
        hf, hb = _lru_scan(xc, w_r_b[l], lru_b_r[l], w_i_b[l], lru_b_i[l], lru_lambda[l], n_lat)
        moe = l % 2 == 1
        outs = _outproj(a_out, m_out, hf, hb, lg, xs, w_out_b[l], mod, ln_g[l, 0][None], ln_b[l, 0][None],
                        alpha, n_lat, router_p[l // 2] if moe else None)
        if moe:
            x1, f_in, topi, gate = outs
            src, scale, pos, tile_expert, n_valid = _dispatch(topi[:, :2], gate[:, :2], moe_tm)
            x_sorted = _row_gather(f_in, src)
            y = _ffn(x_sorted, tile_expert, n_valid, moe_w_gate[l // 2].astype(BF16),
                     moe_w_up[l // 2].astype(BF16), moe_w_down[l // 2].astype(BF16), scale, moe_tm)
            f_parts = _row_gather(y, pos).reshape(2, t, d)
        else:
            x1, f_in = outs
            n_tiles = t // ffn_tm
            y = _ffn(f_in, jnp.zeros((n_tiles,), jnp.int32), jnp.full((1,), n_tiles, jnp.int32),
                     ffn_w_gate[l // 2][None].astype(BF16), ffn_w_up[l // 2][None].astype(BF16),
                     ffn_w_down[l // 2][None].astype(BF16), jnp.ones((t, 1), F32), ffn_tm)
            f_parts = y[None]
        xs = _ln2(x1, f_parts, mod, ln_g[l, 1][None], ln_b[l, 1][None], alpha, n_lat, n_lat if last else t)
    return xs[None]
```

```python
import functools
import math

import jax
import jax.numpy as jnp
import numpy as np
from jax import lax
from jax.experimental import pallas as pl
from jax.experimental.pallas import tpu as pltpu

GRID_W = 64
HEAD_DIM = 128
GQA_HEADS = 8
GQA_KV_HEADS = 2
LRU_WIDTH = 512
LRU_BLOCKS = 4
LRU_BLOCK_W = LRU_WIDTH // LRU_BLOCKS
CONV_WIDTH = 4
LRU_C = 8.0
MLA_HEADS = 4
MLA_Q_RANK = 384
MLA_KV_RANK = 256
MLA_NOPE = 128
MLA_ROPE = 64
MLA_V = 128
N_EXPERTS = 8
ROPE_THETA = 10000.0
EPS = 1e-6
LOG2E = 1.4426950408889634

LANES = 128
MLA_QK_PAD = 2 * LANES
OFF_GQ = 0
OFF_GK = OFF_GQ + GQA_HEADS * HEAD_DIM
OFF_GV = OFF_GK + GQA_KV_HEADS * HEAD_DIM
OFF_LX = OFF_GV + GQA_KV_HEADS * HEAD_DIM
OFF_LG = OFF_LX + LRU_WIDTH
OFF_CQ = OFF_LG + LRU_WIDTH
OFF_CKV = OFF_CQ + MLA_Q_RANK
OFF_KR = OFF_CKV + MLA_KV_RANK
D_IN = OFF_KR + MLA_ROPE
D_IN_PAD = OFF_KR + LANES
D_GQA = GQA_HEADS * HEAD_DIM
D_MLA = MLA_HEADS * MLA_V

ROW_TILE = 256
ATTN_TQ = 512
ATTN_TK = 512
FFN_TM = 640
MOE_TM = 512
FFN_TF_CHOICES = (1024, 512)
GATHER_CHUNK = 256
VMEM_LIMIT = 56 * 1024 * 1024

BF16 = jnp.bfloat16
F32 = jnp.float32


def _params(sem, vmem=VMEM_LIMIT):
    return pltpu.CompilerParams(dimension_semantics=sem, vmem_limit_bytes=vmem)


def _mod_kernel(cond_ref, w_ref, b_ref, o_ref):
    cc = cond_ref[...]
    s = cc * jax.nn.sigmoid(cc)
    w = w_ref[0]
    m0 = jnp.sum(w * s[:, 0:1], axis=0, keepdims=True)
    m1 = jnp.sum(w * s[:, 1:2], axis=0, keepdims=True)
    o_ref[0] = jnp.concatenate([m0, m1], axis=0) + b_ref[0]


def _modulation(cond_t, ada_w, ada_b):
    depth, d, n = ada_w.shape
    tn = 1024 if n % 1024 == 0 else n
    return pl.pallas_call(
        _mod_kernel,
        grid=(depth, n // tn),
        in_specs=[pl.BlockSpec((d, 2), lambda l, j: (0, 0)),
                  pl.BlockSpec((1, d, tn), lambda l, j: (l, 0, j)),
                  pl.BlockSpec((1, 1, tn), lambda l, j: (l, 0, j))],
        out_specs=pl.BlockSpec((1, 2, tn), lambda l, j: (l, 0, j)),
        out_shape=jax.ShapeDtypeStruct((depth, 2, n), F32),
        compiler_params=_params(("arbitrary", "arbitrary")),
        name="modulation",
    )(cond_t, ada_w, ada_b.reshape(depth, 1, n))


def _rms(x, g):
    return x * lax.rsqrt(jnp.mean(x * x, axis=-1, keepdims=True) + EPS) * g


def _inproj_kernel(x_ref, mod_ref, w_ref, qg_ref, kg_ref, mqg_ref, wuq_ref, mkg_ref, wukv_ref,
                   cg_ref, sg_ref, cm_ref, sm_ref,
                   qg_o, kg_o, vg_o, qm_o, km_o, vm_o, lx_o, lg_o):
    shift = mod_ref[0, 0:1, :]
    scale = mod_ref[0, 1:2, :]
    h = x_ref[...] * (1.0 + scale) + shift
    proj = jnp.dot(h.astype(BF16), w_ref[...], preferred_element_type=F32)

    cg = cg_ref[...]
    sg = sg_ref[...]

    def rope_gqa(v):
        return v * cg + pltpu.roll(v, HEAD_DIM // 2, 1) * sg

    q_scale = HEAD_DIM ** -0.5 * LOG2E
    for hh in range(GQA_HEADS):
        q = _rms(proj[:, OFF_GQ + hh * HEAD_DIM: OFF_GQ + (hh + 1) * HEAD_DIM], qg_ref[...])
        qg_o[hh] = (rope_gqa(q) * q_scale).astype(BF16)
    for hh in range(GQA_KV_HEADS):
        k = _rms(proj[:, OFF_GK + hh * HEAD_DIM: OFF_GK + (hh + 1) * HEAD_DIM], kg_ref[...])
        kg_o[hh] = rope_gqa(k).astype(BF16)
        vg_o[hh] = proj[:, OFF_GV + hh * HEAD_DIM: OFF_GV + (hh + 1) * HEAD_DIM].T.astype(BF16)

    lx_o[...] = proj[:, OFF_LX:OFF_LX + LRU_WIDTH]
    lg_o[...] = proj[:, OFF_LG:OFF_LG + LRU_WIDTH]

    cq = _rms(proj[:, OFF_CQ:OFF_CQ + MLA_Q_RANK], mqg_ref[...])
    qm = jnp.dot(cq.astype(BF16), wuq_ref[...], preferred_element_type=F32)
    ckv = _rms(proj[:, OFF_CKV:OFF_CKV + MLA_KV_RANK], mkg_ref[...])
    kv = jnp.dot(ckv.astype(BF16), wukv_ref[...], preferred_element_type=F32)

    cm = cm_ref[...]
    sm = sm_ref[...]

    def rope_mla(v):
        swapped = pltpu.roll(v, MLA_ROPE // 2, 1) + pltpu.roll(v, LANES - MLA_ROPE // 2, 1)
        return v * cm + swapped * sm

    kr = rope_mla(proj[:, OFF_KR:OFF_KR + LANES])
    m_scale = (MLA_NOPE + MLA_ROPE) ** -0.5 * LOG2E
    for hh in range(MLA_HEADS):
        base = hh * MLA_QK_PAD
        qn = qm[:, base:base + MLA_NOPE]
        qr = rope_mla(qm[:, base + MLA_NOPE:base + MLA_QK_PAD])
        qm_o[hh] = (jnp.concatenate([qn, qr], axis=1) * m_scale).astype(BF16)
        kvb = hh * (MLA_NOPE + MLA_V)
        km_o[hh] = jnp.concatenate([kv[:, kvb:kvb + MLA_NOPE], kr], axis=1).astype(BF16)
        vm_o[hh] = kv[:, kvb + MLA_NOPE:kvb + MLA_NOPE + MLA_V].T.astype(BF16)


def _inproj(x, mod, w_in, q_gain, k_gain, mla_q_gain, w_uq, mla_kv_gain, w_ukv, tabs, n_lat):
    t, d = x.shape
    tm = ROW_TILE
    n_lat_tiles = n_lat // tm
    full = lambda a: pl.BlockSpec(a.shape, lambda i: (0,) * a.ndim)
    row = lambda w: pl.BlockSpec((tm, w), lambda i: (i, 0))
    heads = lambda n, w: pl.BlockSpec((n, tm, w), lambda i: (0, i, 0))
    heads_t = lambda n, w: pl.BlockSpec((n, w, tm), lambda i: (0, 0, i))
    cg, sg, cm, sm = tabs
    return pl.pallas_call(
        _inproj_kernel,
        grid=(t // tm,),
        in_specs=[row(d),
                  pl.BlockSpec((1, 6, d), lambda i: (jnp.where(i < n_lat_tiles, 0, 1), 0, 0)),
                  full(w_in), full(q_gain), full(k_gain), full(mla_q_gain), full(w_uq), full(mla_kv_gain),
                  full(w_ukv), row(LANES), row(LANES), row(LANES), row(LANES)],
        out_specs=[heads(GQA_HEADS, HEAD_DIM), heads(GQA_KV_HEADS, HEAD_DIM), heads_t(GQA_KV_HEADS, HEAD_DIM),
                   heads(MLA_HEADS, MLA_QK_PAD), heads(MLA_HEADS, MLA_QK_PAD), heads_t(MLA_HEADS, MLA_V),
                   row(LRU_WIDTH), row(LRU_WIDTH)],
        out_shape=[jax.ShapeDtypeStruct((GQA_HEADS, t, HEAD_DIM), BF16),
                   jax.ShapeDtypeStruct((GQA_KV_HEADS, t, HEAD_DIM), BF16),
                   jax.ShapeDtypeStruct((GQA_KV_HEADS, HEAD_DIM, t), BF16),
                   jax.ShapeDtypeStruct((MLA_HEADS, t, MLA_QK_PAD), BF16),
                   jax.ShapeDtypeStruct((MLA_HEADS, t, MLA_QK_PAD), BF16),
                   jax.ShapeDtypeStruct((MLA_HEADS, MLA_V, t), BF16),
                   jax.ShapeDtypeStruct((t, LRU_WIDTH), F32),
                   jax.ShapeDtypeStruct((t, LRU_WIDTH), F32)],
        compiler_params=_params(("arbitrary",)),
        name="inproj",
    )(x, mod, w_in, q_gain, k_gain, mla_q_gain, w_uq, mla_kv_gain, w_ukv, cg, sg, cm, sm)


ONES_ROWS = 16


def _scores_t(q, k_ref, start, size):
    k = k_ref[0, pl.ds(start, size), :]
    return lax.dot_general(k, q, (((1,), (1,)), ((), ())), preferred_element_type=F32)


def _softmax_step(st, vt_ref, start, size, carry):
    m, acc = carry
    ones = jnp.where(lax.broadcasted_iota(jnp.int32, (ONES_ROWS, size), 0) == 0, 1.0, 0.0).astype(BF16)
    vt = jnp.concatenate([vt_ref[0, :, pl.ds(start, size)], ones], axis=0)
    m_new = jnp.maximum(m, jnp.max(st, axis=0, keepdims=True))
    alpha = jnp.exp2(m - m_new)
    pt = jnp.exp2(st - m_new).astype(BF16)
    acc = alpha * acc + jnp.dot(vt, pt, preferred_element_type=F32)
    return m_new, acc


def _attn_init(tq, dv):
    return jnp.full((1, tq), -1e30, F32), jnp.zeros((dv + ONES_ROWS, tq), F32)


def _attn_finish(carry, o_ref):
    _, acc = carry
    dv = acc.shape[0] - ONES_ROWS
    o_ref[...] = (acc[:dv, :] / acc[dv:dv + 1, :]).T.astype(o_ref.dtype)


def _attn_latent_kernel(n_main, tk, tail, unroll, q_ref, k_ref, vt_ref, o_ref, s_ref):
    hp, tq = q_ref.shape[0], q_ref.shape[1]
    dv = vt_ref.shape[1]
    qs = [q_ref[h] for h in range(hp)]

    def put(h, slot, st):
        s_ref[h, slot, :, 0:tq] = st

    def get(h, slot):
        return s_ref[h, slot, :, 0:tq]

    def body(jj, carries):
        c0 = pl.multiple_of(jj * (2 * tk), 2 * tk)
        carries = list(carries)
        for h in range(hp):
            put(h, 1, _scores_t(qs[h], k_ref, c0 + tk, tk))
            carries[h] = _softmax_step(get(h, 0), vt_ref, c0, tk, carries[h])
        for h in range(hp):
            put(h, 0, _scores_t(qs[h], k_ref, c0 + 2 * tk, tk))
            carries[h] = _softmax_step(get(h, 1), vt_ref, c0 + tk, tk, carries[h])
        return tuple(carries)

    carries = tuple(_attn_init(tq, dv) for _ in range(hp))
    for h in range(hp):
        put(h, 0, _scores_t(qs[h], k_ref, 0, tk))
    n_loop = (n_main - 1) // 2
    carries = list(lax.fori_loop(0, n_loop, body, carries, unroll=unroll))
    c0 = 2 * n_loop * tk
    two_left = n_main - 2 * n_loop == 2
    for h in range(hp):
        if two_left:
            put(h, 1, _scores_t(qs[h], k_ref, c0 + tk, tk))
        s_tail = _scores_t(qs[h], k_ref, n_main * tk, tail) if tail else None
        carries[h] = _softmax_step(get(h, 0), vt_ref, c0, tk, carries[h])
        if two_left:
            carries[h] = _softmax_step(get(h, 1), vt_ref, c0 + tk, tk, carries[h])
        if tail:
            carries[h] = _softmax_step(s_tail, vt_ref, n_main * tk, tail, carries[h])
        _attn_finish(carries[h], o_ref.at[:, h * dv:(h + 1) * dv])


def _attn_context_kernel(q_ref, k_ref, vt_ref, o_ref):
    n = k_ref.shape[1]
    carry = _attn_init(q_ref.shape[1], vt_ref.shape[1])
    carry = _softmax_step(_scores_t(q_ref[0], k_ref, 0, n), vt_ref, 0, n, carry)
    _attn_finish(carry, o_ref)


def _attention(q, k, vt, n_lat, group, name, tk=ATTN_TK, unroll=2, hp=1, pad=0):
    n_heads, t, dk = q.shape
    dv = vt.shape[1]
    n_ctx = t - n_lat
    tq = ATTN_TQ if n_lat % ATTN_TQ == 0 else ROW_TILE
    if t % tk == 0:
        n_main, tail = t // tk, 0
    else:
        tk = tk if n_lat % tk == 0 else ROW_TILE
        n_main, tail = n_lat // tk, n_ctx
    assert n_main * tk + tail == t and group % hp == 0 and n_heads % hp == 0
    lat = pl.pallas_call(
        functools.partial(_attn_latent_kernel, n_main, tk, tail, unroll),
        grid=(n_heads // hp, n_lat // tq),
        in_specs=[pl.BlockSpec((hp, tq, dk), lambda h, i: (h, i, 0)),
                  pl.BlockSpec((1, t, dk), lambda h, i: ((h * hp) // group, 0, 0)),
                  pl.BlockSpec((1, dv, t), lambda h, i: ((h * hp) // group, 0, 0))],
        out_specs=pl.BlockSpec((tq, hp * dv), lambda h, i: (i, h)),
        out_shape=jax.ShapeDtypeStruct((n_lat, n_heads * dv), BF16),
        scratch_shapes=[pltpu.VMEM((hp, 2, tk, tq + pad), F32)],
        compiler_params=_params(("arbitrary", "arbitrary")),
        name=name + "_latent",
    )(q, k, vt)
    cblk = n_lat // n_ctx
    ctx = pl.pallas_call(
        _attn_context_kernel,
        grid=(n_heads,),
        in_specs=[pl.BlockSpec((1, n_ctx, dk), lambda h: (h, cblk, 0)),
                  pl.BlockSpec((1, n_ctx, dk), lambda h: (h // group, cblk, 0)),
                  pl.BlockSpec((1, dv, n_ctx), lambda h: (h // group, 0, cblk))],
        out_specs=pl.BlockSpec((n_ctx, dv), lambda h: (0, h)),
        out_shape=jax.ShapeDtypeStruct((n_ctx, n_heads * dv), BF16),
        compiler_params=_params(("arbitrary",)),
        name=name + "_context",
    )(q, k, vt)
    return lat, ctx


def _conv_kernel(n_lat, t_total, xp_ref, xc_ref, xn_ref, w_ref, b_ref, o_ref):
    i = pl.program_id(0)
    r = xc_ref.shape[0]
    halo = 8
    ext = jnp.concatenate([xp_ref[r - halo:r, :], xc_ref[...], xn_ref[0:halo, :]], axis=0)
    rows = i * r + lax.broadcasted_iota(jnp.int32, (r, 1), 0)
    in_ctx = rows >= n_lat
    seg_lo = jnp.where(in_ctx, n_lat, 0)
    seg_hi = jnp.where(in_ctx, t_total, n_lat)
    left = CONV_WIDTH // 2
    out = jnp.broadcast_to(b_ref[...], (r, xc_ref.shape[1]))
    for tap in range(CONV_WIDTH):
        off = tap - left
        shifted = ext if off == 0 else pltpu.roll(ext, (-off) % (r + 2 * halo), 0)
        xs = shifted[halo:halo + r, :]
        src = rows + off
        valid = jnp.logical_and(src >= seg_lo, src < seg_hi)
        out = out + jnp.where(valid, xs, 0.0) * w_ref[tap:tap + 1, :]
    o_ref[...] = out


def _conv(lx, conv_w, conv_b, n_lat):
    t, w = lx.shape
    r = ROW_TILE
    nt = t // r
    return pl.pallas_call(
        functools.partial(_conv_kernel, n_lat, t),
        grid=(nt,),
        in_specs=[pl.BlockSpec((r, w), lambda i: (jnp.maximum(i - 1, 0), 0)),
                  pl.BlockSpec((r, w), lambda i: (i, 0)),
                  pl.BlockSpec((r, w), lambda i: (jnp.minimum(i + 1, nt - 1), 0)),
                  pl.BlockSpec(conv_w.shape, lambda i: (0, 0)),
                  pl.BlockSpec((1, w), lambda i: (0, 0))],
        out_specs=pl.BlockSpec((r, w), lambda i: (i, 0)),
        out_shape=jax.ShapeDtypeStruct((t, w), F32),
        compiler_params=_params(("arbitrary",)),
        name="lru_conv",
    )(lx, lx, lx, conv_w, conv_b.reshape(1, w))


def _lru_coeffs(x, d, wr_ref, br_ref, wi_ref, bi_ref, lam_ref):
    xb = x.astype(BF16)
    r_parts, i_parts = [], []
    for n in range(LRU_BLOCKS):
        xs = xb[:, n * LRU_BLOCK_W:(n + 1) * LRU_BLOCK_W]
        r_parts.append(jnp.dot(xs, wr_ref[d, n], preferred_element_type=F32))
        i_parts.append(jnp.dot(xs, wi_ref[d, n], preferred_element_type=F32))
    r = jax.nn.sigmoid(jnp.concatenate(r_parts, axis=1) + br_ref[d:d + 1, :])
    ig = jax.nn.sigmoid(jnp.concatenate(i_parts, axis=1) + bi_ref[d:d + 1, :])
    neg_lam = -lam_ref[d:d + 1, :]
    softplus = jnp.maximum(neg_lam, 0.0) + jnp.log1p(jnp.exp(-jnp.abs(neg_lam)))
    log_a = -LRU_C * r * softplus
    a = jnp.exp(log_a)
    u = jnp.sqrt(-jnp.tanh(log_a) * (a * a + 1.0)) * ig * x
    return a, u


def _tile_scan(a, u, reverse):
    r = a.shape[0]
    rows = lax.broadcasted_iota(jnp.int32, (r, 1), 0)
    d = 1
    while d < r:
        if reverse:
            a_sh = pltpu.roll(a, r - d, 0)
            u_sh = pltpu.roll(u, r - d, 0)
            ok = rows < r - d
        else:
            a_sh = pltpu.roll(a, d, 0)
            u_sh = pltpu.roll(u, d, 0)
            ok = rows >= d
        u = jnp.where(ok, a * u_sh + u, u)
        a = jnp.where(ok, a * a_sh, a)
        d *= 2
    return a, u


def _scan_kernel(nc_tiles, xf_ref, xb_ref, wr_ref, br_ref, wi_ref, bi_ref, lam_ref, hf_ref, hb_ref, carry_ref):
    s = pl.program_id(0)
    r = xf_ref.shape[0]

    @pl.when(s == 0)
    def _():
        carry_ref[...] = jnp.zeros_like(carry_ref)

    del nc_tiles
    a, u = _lru_coeffs(xf_ref[...], 0, wr_ref, br_ref, wi_ref, bi_ref, lam_ref)
    a, u = _tile_scan(a, u, False)
    h = a * carry_ref[0:1, :] + u
    hf_ref[...] = h
    carry_ref[0:1, :] = h[r - 1:r, :]

    a, u = _lru_coeffs(xb_ref[...], 1, wr_ref, br_ref, wi_ref, bi_ref, lam_ref)
    a, u = _tile_scan(a, u, True)
    h = a * carry_ref[1:2, :] + u
    hb_ref[...] = h
    carry_ref[1:2, :] = h[0:1, :]


def _lru_scan(xc, w_r, b_r, w_i, b_i, lam, n_lat):
    t, w = xc.shape
    r = ROW_TILE
    nt, nl = t // r, n_lat // r
    nc = nt - nl
    fwd = lambda s: (jnp.where(s < nc, nl + s, s - nc), 0)
    bwd = lambda s: (jnp.where(s < nc, nt - 1 - s, nl - 1 - (s - nc)), 0)
    full = lambda a: pl.BlockSpec(a.shape, lambda s: (0,) * a.ndim)
    return pl.pallas_call(
        functools.partial(_scan_kernel, nc),
        grid=(nt,),
        in_specs=[pl.BlockSpec((r, w), fwd), pl.BlockSpec((r, w), bwd),
                  full(w_r), full(b_r), full(w_i), full(b_i), full(lam)],
        out_specs=[pl.BlockSpec((r, w), fwd), pl.BlockSpec((r, w), bwd)],
        out_shape=[jax.ShapeDtypeStruct((t, w), F32), jax.ShapeDtypeStruct((t, w), F32)],
        scratch_shapes=[pltpu.VMEM((8, w), F32)],
        compiler_params=_params(("arbitrary",)),
        name="lru_scan",
    )(xc, xc, w_r, b_r, w_i, b_i, lam)


def _layer_norm(y, g, b):
    mu = jnp.mean(y, axis=-1, keepdims=True)
    yc = y - mu
    var = jnp.mean(yc * yc, axis=-1, keepdims=True)
    return yc * lax.rsqrt(var + EPS) * g + b


def _gelu_tanh(x):
    return 0.5 * x * (1.0 + jnp.tanh(math.sqrt(2.0 / math.pi) * (x + 0.044715 * (x * x * x))))


def _outproj_kernel(alpha, with_router, n_lat_tiles, al_ref, ac_ref, ml_ref, mc_ref, hf_ref, hb_ref, lg_ref, x_ref,
                    w_ref, mod_ref, g_ref, b_ref, *rest):
    if with_router:
        router_ref, x1_o, fin_o, topi_o, gate_o = rest
    else:
        x1_o, fin_o = rest
    is_lat = pl.program_id(0) < n_lat_tiles
    a = jnp.where(is_lat, al_ref[...], ac_ref[...])
    m = jnp.where(is_lat, ml_ref[...], mc_ref[...])
    rec = _gelu_tanh(lg_ref[...]) * (hf_ref[...] + hb_ref[...])
    mix = jnp.dot(a, w_ref[0:D_GQA, :], preferred_element_type=F32)
    mix += jnp.dot(rec.astype(BF16), w_ref[D_GQA:D_GQA + LRU_WIDTH, :], preferred_element_type=F32)
    mix += jnp.dot(m, w_ref[D_GQA + LRU_WIDTH:, :], preferred_element_type=F32)
    y = alpha * x_ref[...] + mod_ref[0, 2:3, :] * mix
    x1 = _layer_norm(y, g_ref[...], b_ref[...])
    x1_o[...] = x1
    f_in = x1 * (1.0 + mod_ref[0, 4:5, :]) + mod_ref[0, 3:4, :]
    fin_o[...] = f_in
    if with_router:
        f_hi = f_in.astype(BF16)
        f_lo = (f_in - f_hi.astype(F32)).astype(BF16)
        r = router_ref[...]
        r_hi = r.astype(BF16)
        r_lo = (r - r_hi.astype(F32)).astype(BF16)
        logits = (jnp.dot(f_hi, r_hi, preferred_element_type=F32) + jnp.dot(f_hi, r_lo, preferred_element_type=F32)
                  + jnp.dot(f_lo, r_hi, preferred_element_type=F32))
        lane = lax.broadcasted_iota(jnp.int32, logits.shape, 1).astype(F32)
        neg = jnp.float32(-jnp.inf)
        lgt = jnp.where(lane < N_EXPERTS, logits, neg)
        v1 = jnp.max(lgt, axis=-1, keepdims=True)
        i1 = jnp.min(jnp.where(lgt == v1, lane, float(LANES)), axis=-1, keepdims=True)
        lgt2 = jnp.where(lane == i1, neg, lgt)
        v2 = jnp.max(lgt2, axis=-1, keepdims=True)
        i2 = jnp.min(jnp.where(lgt2 == v2, lane, float(LANES)), axis=-1, keepdims=True)
        e2 = jnp.exp(v2 - v1)
        den = 1.0 + e2
        topi_o[...] = jnp.where(lane == 0.0, i1, jnp.where(lane == 1.0, i2, 0.0)).astype(jnp.int32)
        gate_o[...] = jnp.where(lane == 0.0, 1.0 / den, jnp.where(lane == 1.0, e2 / den, 0.0))


def _outproj(a, m, hf, hb, lg, x, w_out, mod, ln_g, ln_b, alpha, n_lat, router):
    t, d = x.shape
    tm = ROW_TILE
    n_lat_tiles = n_lat // tm
    row = lambda w: pl.BlockSpec((tm, w), lambda i: (i, 0))
    lat_row = lambda w: pl.BlockSpec((tm, w), lambda i: (jnp.minimum(i, n_lat_tiles - 1), 0))
    ctx_row = lambda w: pl.BlockSpec((tm, w), lambda i: (jnp.maximum(i - n_lat_tiles, 0), 0))
    full = lambda arr: pl.BlockSpec(arr.shape, lambda i: (0,) * arr.ndim)
    in_specs = [lat_row(D_GQA), ctx_row(D_GQA), lat_row(D_MLA), ctx_row(D_MLA),
                row(LRU_WIDTH), row(LRU_WIDTH), row(LRU_WIDTH), row(d),
                full(w_out), pl.BlockSpec((1, 6, d), lambda i: (jnp.where(i < n_lat_tiles, 0, 1), 0, 0)),
                full(ln_g), full(ln_b)]
    out_specs = [row(d), row(d)]
    out_shape = [jax.ShapeDtypeStruct((t, d), F32), jax.ShapeDtypeStruct((t, d), F32)]
    args = [a[0], a[1], m[0], m[1], hf, hb, lg, x, w_out, mod, ln_g, ln_b]
    if router is not None:
        in_specs.append(full(router))
        args.append(router)
        out_specs += [row(LANES), row(LANES)]
        out_shape += [jax.ShapeDtypeStruct((t, LANES), jnp.int32), jax.ShapeDtypeStruct((t, LANES), F32)]
    return pl.pallas_call(
        functools.partial(_outproj_kernel, alpha, router is not None, n_lat_tiles),
        grid=(t // tm,),
        in_specs=in_specs, out_specs=out_specs, out_shape=out_shape,
        compiler_params=_params(("arbitrary",)),
        name="outproj",
    )(*args)


def _ffn_kernel(te_ref, nv_ref, x_ref, wg_ref, wu_ref, wd_ref, sc_ref, o_ref, acc_ref):
    del te_ref
    i, j = pl.program_id(0), pl.program_id(1)

    @pl.when(j == 0)
    def _():
        acc_ref[...] = jnp.zeros_like(acc_ref)

    @pl.when(i < nv_ref[0])
    def _():
        xb = x_ref[...].astype(BF16)
        g = jnp.dot(xb, wg_ref[0], preferred_element_type=F32)
        u = jnp.dot(xb, wu_ref[0], preferred_element_type=F32)
        hidden = (g * jax.nn.sigmoid(g)) * u
        acc_ref[...] += jnp.dot(hidden.astype(BF16), wd_ref[0], preferred_element_type=F32)

    @pl.when(j == pl.num_programs(1) - 1)
    def _():
        o_ref[...] = acc_ref[...] * sc_ref[...]


def _ffn(x, tile_expert, n_valid, w_gate, w_up, w_down, row_scale, tm):
    m_rows, d = x.shape
    f = w_gate.shape[2]
    tf = next((c for c in FFN_TF_CHOICES if f % c == 0), f)
    nj = f // tf
    col = lambda i, j, nv: jnp.where(i < nv[0], j, nj - 1)
    grid_spec = pltpu.PrefetchScalarGridSpec(
        num_scalar_prefetch=2,
        grid=(m_rows // tm, nj),
        in_specs=[pl.BlockSpec((tm, d), lambda i, j, te, nv: (i, 0)),
                  pl.BlockSpec((1, d, tf), lambda i, j, te, nv: (te[i], 0, col(i, j, nv))),
                  pl.BlockSpec((1, d, tf), lambda i, j, te, nv: (te[i], 0, col(i, j, nv))),
                  pl.BlockSpec((1, tf, d), lambda i, j, te, nv: (te[i], col(i, j, nv), 0)),
                  pl.BlockSpec((tm, 1), lambda i, j, te, nv: (i, 0))],
        out_specs=pl.BlockSpec((tm, d), lambda i, j, te, nv: (i, 0)),
        scratch_shapes=[pltpu.VMEM((tm, d), F32)],
    )
    return pl.pallas_call(
        _ffn_kernel,
        grid_spec=grid_spec,
        out_shape=jax.ShapeDtypeStruct((m_rows, d), F32),
        compiler_params=_params(("arbitrary", "arbitrary")),
        name="ffn",
    )(tile_expert, n_valid, x, w_gate, w_up, w_down, row_scale)


def _gather_kernel(idx_ref, src_ref, dst_ref, sem):
    base = pl.program_id(0) * GATHER_CHUNK

    def issue(r, carry):
        pltpu.make_async_copy(src_ref.at[pl.ds(idx_ref[base + r], 1)], dst_ref.at[pl.ds(r, 1)], sem).start()
        return carry

    lax.fori_loop(0, GATHER_CHUNK, issue, 0, unroll=8)
    pltpu.make_async_copy(src_ref.at[pl.ds(0, GATHER_CHUNK)], dst_ref, sem).wait()


def _row_gather(src, idx):
    m_rows = idx.shape[0]
    assert m_rows % GATHER_CHUNK == 0
    grid_spec = pltpu.PrefetchScalarGridSpec(
        num_scalar_prefetch=1,
        grid=(m_rows // GATHER_CHUNK,),
        in_specs=[pl.BlockSpec(memory_space=pl.ANY)],
        out_specs=pl.BlockSpec((GATHER_CHUNK, src.shape[1]), lambda c, idx: (c, 0)),
        scratch_shapes=[pltpu.SemaphoreType.DMA(())],
    )
    return pl.pallas_call(
        _gather_kernel,
        grid_spec=grid_spec,
        out_shape=jax.ShapeDtypeStruct((m_rows, src.shape[1]), src.dtype),
        compiler_params=_params(("arbitrary",)),
        name="row_gather",
    )(idx, src)


def _ln2_kernel(alpha, n_parts, x_ref, f_ref, mod_ref, g_ref, b_ref, o_ref):
    f = f_ref[0]
    for p in range(1, n_parts):
        f = f + f_ref[p]
    y = alpha * x_ref[...] + mod_ref[0, 5:6, :] * f
    o_ref[...] = _layer_norm(y, g_ref[...], b_ref[...])


def _ln2(x1, f_parts, mod, ln_g, ln_b, alpha, n_lat, n_rows):
    n_parts, _, d = f_parts.shape
    tm = ROW_TILE
    n_lat_tiles = n_lat // tm
    full = lambda arr: pl.BlockSpec(arr.shape, lambda i: (0,) * arr.ndim)
    return pl.pallas_call(
        functools.partial(_ln2_kernel, alpha, n_parts),
        grid=(n_rows // tm,),
        in_specs=[pl.BlockSpec((tm, d), lambda i: (i, 0)),
                  pl.BlockSpec((n_parts, tm, d), lambda i: (0, i, 0)),
                  pl.BlockSpec((1, 6, d), lambda i: (jnp.where(i < n_lat_tiles, 0, 1), 0, 0)),
                  full(ln_g), full(ln_b)],
        out_specs=pl.BlockSpec((tm, d), lambda i: (i, 0)),
        out_shape=jax.ShapeDtypeStruct((n_rows, d), F32),
        compiler_params=_params(("arbitrary",)),
        name="ln2",
    )(x1, f_parts, mod, ln_g, ln_b)


def _rope_tables(n_lat, n_ctx):
    rows = n_lat // GRID_W

    def angles(rot_dim):
        quarter = rot_dim // 4
        inv_freq = ROPE_THETA ** (-jnp.arange(quarter, dtype=F32) / quarter)
        row = jnp.repeat(jnp.arange(rows, dtype=F32), GRID_W)
        col = jnp.tile(jnp.arange(GRID_W, dtype=F32), rows)
        ang = jnp.concatenate([row[:, None] * inv_freq, col[:, None] * inv_freq], axis=-1)
        return jnp.cos(ang), jnp.sin(ang)

    def with_ctx(tab, fill):
        return jnp.concatenate([tab, jnp.full((n_ctx, tab.shape[1]), fill, F32)], axis=0)

    cos_g, sin_g = angles(HEAD_DIM)
    cg = with_ctx(jnp.concatenate([cos_g, cos_g], axis=1), 1.0)
    sg = with_ctx(jnp.concatenate([-sin_g, sin_g], axis=1), 0.0)
    cos_m, sin_m = angles(MLA_ROPE)
    zeros = jnp.zeros((n_lat, LANES - MLA_ROPE), F32)
    cm = with_ctx(jnp.concatenate([cos_m, cos_m, zeros], axis=1), 1.0)
    sm = with_ctx(jnp.concatenate([-sin_m, sin_m, zeros], axis=1), 0.0)
    return cg, sg, cm, sm


def _pad_w_uq(w_uq):
    depth, rank, _ = w_uq.shape
    w = w_uq.reshape(depth, rank, MLA_HEADS, MLA_NOPE + MLA_ROPE)
    w = jnp.pad(w, ((0, 0), (0, 0), (0, 0), (0, MLA_QK_PAD - MLA_NOPE - MLA_ROPE)))
    return w.reshape(depth, rank, MLA_HEADS * MLA_QK_PAD)


def _dispatch(top_i, gates, tm):
    t = top_i.shape[0]
    flat_e = top_i.T.reshape(-1)
    flat_g = gates.T.reshape(-1)
    onehot = (flat_e[:, None] == jnp.arange(N_EXPERTS, dtype=jnp.int32)[None, :]).astype(jnp.int32)
    rank = jnp.sum((jnp.cumsum(onehot, axis=0) - onehot) * onehot, axis=1)
    counts = jnp.sum(onehot, axis=0)
    ptiles = (counts + tm - 1) // tm
    tile_end = jnp.cumsum(ptiles)
    tile_start = tile_end - ptiles
    pos = (tile_start[flat_e] * tm + rank).astype(jnp.int32)
    n_tiles = (2 * t) // tm + N_EXPERTS
    m_pad = n_tiles * tm
    src = jnp.zeros((m_pad,), jnp.int32).at[pos].set(jnp.tile(jnp.arange(t, dtype=jnp.int32), 2))
    scale = jnp.zeros((m_pad,), F32).at[pos].set(flat_g)
    tile_ids = jnp.arange(n_tiles, dtype=jnp.int32)
    last_used = jnp.max(jnp.where(ptiles > 0, jnp.arange(N_EXPERTS, dtype=jnp.int32), 0))
    tile_expert = jnp.minimum(jnp.sum((tile_end[None, :] <= tile_ids[:, None]).astype(jnp.int32), axis=1),
                              last_used).astype(jnp.int32)
    n_valid = tile_end[-1:].astype(jnp.int32)
    return src, scale.reshape(m_pad, 1), pos, tile_expert, n_valid


def kernel(x, c, ctx, c_ctx, ada_w, ada_b, ln_g, ln_b, w_in, w_out, gqa_q_gain, gqa_k_gain, lru_conv_w, lru_conv_b, lru_w_r, lru_b_r, lru_w_i, lru_b_i, lru_lambda, mla_q_gain, mla_w_uq, mla_kv_gain, mla_w_ukv, ffn_w_gate, ffn_w_up, ffn_w_down, moe_router, moe_w_gate, moe_w_up, moe_w_down):
    batch, n_lat, d = x.shape
    n_ctx = ctx.shape[1]
    depth = ada_w.shape[0]
    assert batch == 1 and ctx.shape[0] == 1
    assert n_lat % ROW_TILE == 0 and n_ctx % ROW_TILE == 0 and n_lat % n_ctx == 0 and n_lat % GRID_W == 0
    t = n_lat + n_ctx
    alpha = (2 * depth) ** 0.25

    xs = jnp.concatenate([x[0], ctx[0]], axis=0)
    cond_t = jnp.stack([c[0], c_ctx], axis=1)
    mods = _modulation(cond_t, ada_w, ada_b).reshape(depth, 2, 6, d)
    tabs = _rope_tables(n_lat, n_ctx)

    w_in_b = jnp.pad(w_in, ((0, 0), (0, 0), (0, D_IN_PAD - D_IN))).astype(BF16)
    w_out_b = w_out.astype(BF16)
    w_uq_b = _pad_w_uq(mla_w_uq).astype(BF16)
    w_ukv_b = mla_w_ukv.astype(BF16)
    w_r_b = lru_w_r.astype(BF16)
    w_i_b = lru_w_i.astype(BF16)
    router_p = jnp.pad(moe_router, ((0, 0), (0, 0), (0, LANES - N_EXPERTS)))
    ffn_tm = FFN_TM if t % FFN_TM == 0 else ROW_TILE
    moe_tm = MOE_TM if (2 * t) % MOE_TM == 0 else ROW_TILE
    n_moe, d_exp = moe_w_gate.shape[0], moe_w_gate.shape[3]
    dense_w = (ffn_w_gate.astype(BF16), ffn_w_up.astype(BF16), ffn_w_down.astype(BF16))
    moe_w = (moe_w_gate.astype(BF16).reshape(n_moe * N_EXPERTS, d, d_exp),
             moe_w_up.astype(BF16).reshape(n_moe * N_EXPERTS, d, d_exp),
             moe_w_down.astype(BF16).reshape(n_moe * N_EXPERTS, d_exp, d))
    uni_tk = 640 if t % 640 == 0 else ATTN_TK
    gqa_cfg = [dict(), dict(pad=LANES), dict(hp=2), dict(tk=uni_tk, unroll=3)]
    mla_cfg = [dict(), dict(pad=LANES), dict(tk=uni_tk, unroll=3), dict(tk=uni_tk, unroll=3, pad=LANES)]

    for l in range(depth):
        last = l == depth - 1
        mod = mods[l]
        qg, kg, vg, qm, km, vm, lx, lg = _inproj(
            xs, mod, w_in_b[l], gqa_q_gain[l][None], gqa_k_gain[l][None], mla_q_gain[l][None], w_uq_b[l],
            mla_kv_gain[l][None], w_ukv_b[l], tabs, n_lat)
        a_out = _attention(qg, kg, vg, n_lat, GQA_HEADS // GQA_KV_HEADS, "gqa", **gqa_cfg[l % 4])
        m_out = _attention(qm, km, vm, n_lat, 1, "mla", **mla_cfg[l % 4])
        xc = _conv(lx, lru_conv_w[l], lru_conv_b[l], n_lat)
        hf, hb = _lru_scan(xc, w_r_b[l], lru_b_r[l], w_i_b[l], lru_b_i[l], lru_lambda[l], n_lat)
        moe = l % 2 == 1
        outs = _outproj(a_out, m_out, hf, hb, lg, xs, w_out_b[l], mod, ln_g[l, 0][None], ln_b[l, 0][None],
                        alpha, n_lat, router_p[l // 2] if moe else None)
        if moe:
            x1, f_in, topi, gate = outs
            src, scale, pos, tile_expert, n_valid = _dispatch(topi[:, :2], gate[:, :2], moe_tm)
            x_sorted = _row_gather(f_in, src)
            y = _ffn(x_sorted, tile_expert + (l // 2) * N_EXPERTS, n_valid, *moe_w, scale, moe_tm)
            f_parts = _row_gather(y, pos).reshape(2, t, d)
        else:
            x1, f_in = outs
            n_tiles = t // ffn_tm
            y = _ffn(f_in, jnp.full((n_tiles,), l // 2, jnp.int32), jnp.full((1,), n_tiles, jnp.int32),
                     *dense_w, jnp.ones((t, 1), F32), ffn_tm)
            f_parts = y[None]
        xs = _ln2(x1, f_parts, mod, ln_g[l, 1][None], ln_b[l, 1][None], alpha, n_lat, n_lat if last else t)
    return xs[None]
```

```python
import functools
import math

import jax
import jax.numpy as jnp
import numpy as np
from jax import lax
from jax.experimental import pallas as pl
from jax.experimental.pallas import tpu as pltpu

GRID_W = 64
HEAD_DIM = 128
GQA_HEADS = 8
GQA_KV_HEADS = 2
LRU_WIDTH = 512
LRU_BLOCKS = 4
LRU_BLOCK_W = LRU_WIDTH // LRU_BLOCKS
CONV_WIDTH = 4
LRU_C = 8.0
MLA_HEADS = 4
MLA_Q_RANK = 384
MLA_KV_RANK = 256
MLA_NOPE = 128
MLA_ROPE = 64
MLA_V = 128
N_EXPERTS = 8
ROPE_THETA = 10000.0
EPS = 1e-6
LOG2E = 1.4426950408889634

LANES = 128
MLA_QK_PAD = 2 * LANES
OFF_GQ = 0
OFF_GK = OFF_GQ + GQA_HEADS * HEAD_DIM
OFF_GV = OFF_GK + GQA_KV_HEADS * HEAD_DIM
OFF_LX = OFF_GV + GQA_KV_HEADS * HEAD_DIM
OFF_LG = OFF_LX + LRU_WIDTH
OFF_CQ = OFF_LG + LRU_WIDTH
OFF_CKV = OFF_CQ + MLA_Q_RANK
OFF_KR = OFF_CKV + MLA_KV_RANK
D_IN = OFF_KR + MLA_ROPE
D_IN_PAD = OFF_KR + LANES
D_GQA = GQA_HEADS * HEAD_DIM
D_MLA = MLA_HEADS * MLA_V

ROW_TILE = 256
ATTN_TQ = 512
ATTN_TK = 512
FFN_TM = 640
MOE_TM = 512
FFN_TF_CHOICES = (1024, 512)
GATHER_CHUNK = 512
VMEM_LIMIT = 56 * 1024 * 1024

BF16 = jnp.bfloat16
F32 = jnp.float32


def _params(sem, vmem=VMEM_LIMIT):
    return pltpu.CompilerParams(dimension_semantics=sem, vmem_limit_bytes=vmem)


def _mod_kernel(cond_ref, w_ref, b_ref, o_ref):
    cc = cond_ref[...]
    s = cc * jax.nn.sigmoid(cc)
    w = w_ref[0]
    m0 = jnp.sum(w * s[:, 0:1], axis=0, keepdims=True)
    m1 = jnp.sum(w * s[:, 1:2], axis=0, keepdims=True)
    o_ref[0] = jnp.concatenate([m0, m1], axis=0) + b_ref[0]


def _modulation(cond_t, ada_w, ada_b):
    depth, d, n = ada_w.shape
    tn = 1024 if n % 1024 == 0 else n
    return pl.pallas_call(
        _mod_kernel,
        grid=(depth, n // tn),
        in_specs=[pl.BlockSpec((d, 2), lambda l, j: (0, 0)),
                  pl.BlockSpec((1, d, tn), lambda l, j: (l, 0, j)),
                  pl.BlockSpec((1, 1, tn), lambda l, j: (l, 0, j))],
        out_specs=pl.BlockSpec((1, 2, tn), lambda l, j: (l, 0, j)),
        out_shape=jax.ShapeDtypeStruct((depth, 2, n), F32),
        compiler_params=_params(("arbitrary", "arbitrary")),
        name="modulation",
    )(cond_t, ada_w, ada_b.reshape(depth, 1, n))


def _rms(x, g):
    return x * lax.rsqrt(jnp.mean(x * x, axis=-1, keepdims=True) + EPS) * g


def _inproj_kernel(x_ref, mod_ref, w_ref, qg_ref, kg_ref, mqg_ref, wuq_ref, mkg_ref, wukv_ref,
                   cg_ref, sg_ref, cm_ref, sm_ref,
                   qg_o, kg_o, vg_o, qm_o, km_o, vm_o, lx_o, lg_o):
    shift = mod_ref[0, 0:1, :]
    scale = mod_ref[0, 1:2, :]
    h = x_ref[...] * (1.0 + scale) + shift
    proj = jnp.dot(h.astype(BF16), w_ref[...], preferred_element_type=F32)

    cg = cg_ref[...]
    sg = sg_ref[...]

    def rope_gqa(v):
        return v * cg + pltpu.roll(v, HEAD_DIM // 2, 1) * sg

    q_scale = HEAD_DIM ** -0.5 * LOG2E
    for hh in range(GQA_HEADS):
        q = _rms(proj[:, OFF_GQ + hh * HEAD_DIM: OFF_GQ + (hh + 1) * HEAD_DIM], qg_ref[...])
        qg_o[hh] = (rope_gqa(q) * q_scale).astype(BF16)
    for hh in range(GQA_KV_HEADS):
        k = _rms(proj[:, OFF_GK + hh * HEAD_DIM: OFF_GK + (hh + 1) * HEAD_DIM], kg_ref[...])
        kg_o[hh] = rope_gqa(k).astype(BF16)
        vg_o[hh] = proj[:, OFF_GV + hh * HEAD_DIM: OFF_GV + (hh + 1) * HEAD_DIM].T.astype(BF16)

    lx_o[...] = proj[:, OFF_LX:OFF_LX + LRU_WIDTH]
    lg_o[...] = proj[:, OFF_LG:OFF_LG + LRU_WIDTH]

    cq = _rms(proj[:, OFF_CQ:OFF_CQ + MLA_Q_RANK], mqg_ref[...])
    qm = jnp.dot(cq.astype(BF16), wuq_ref[...], preferred_element_type=F32)
    ckv = _rms(proj[:, OFF_CKV:OFF_CKV + MLA_KV_RANK], mkg_ref[...])
    kv = jnp.dot(ckv.astype(BF16), wukv_ref[...], preferred_element_type=F32)

    cm = cm_ref[...]
    sm = sm_ref[...]

    def rope_mla(v):
        swapped = pltpu.roll(v, MLA_ROPE // 2, 1) + pltpu.roll(v, LANES - MLA_ROPE // 2, 1)
        return v * cm + swapped * sm

    kr = rope_mla(proj[:, OFF_KR:OFF_KR + LANES])
    m_scale = (MLA_NOPE + MLA_ROPE) ** -0.5 * LOG2E
    for hh in range(MLA_HEADS):
        base = hh * MLA_QK_PAD
        qn = qm[:, base:base + MLA_NOPE]
        qr = rope_mla(qm[:, base + MLA_NOPE:base + MLA_QK_PAD])
        qm_o[hh] = (jnp.concatenate([qn, qr], axis=1) * m_scale).astype(BF16)
        kvb = hh * (MLA_NOPE + MLA_V)
        km_o[hh] = jnp.concatenate([kv[:, kvb:kvb + MLA_NOPE], kr], axis=1).astype(BF16)
        vm_o[hh] = kv[:, kvb + MLA_NOPE:kvb + MLA_NOPE + MLA_V].T.astype(BF16)


def _inproj(x, mod, w_in, q_gain, k_gain, mla_q_gain, w_uq, mla_kv_gain, w_ukv, tabs, n_lat):
    t, d = x.shape
    tm = ROW_TILE
    n_lat_tiles = n_lat // tm
    full = lambda a: pl.BlockSpec(a.shape, lambda i: (0,) * a.ndim)
    row = lambda w: pl.BlockSpec((tm, w), lambda i: (i, 0))
    heads = lambda n, w: pl.BlockSpec((n, tm, w), lambda i: (0, i, 0))
    heads_t = lambda n, w: pl.BlockSpec((n, w, tm), lambda i: (0, 0, i))
    cg, sg, cm, sm = tabs
    return pl.pallas_call(
        _inproj_kernel,
        grid=(t // tm,),
        in_specs=[row(d),
                  pl.BlockSpec((1, 6, d), lambda i: (jnp.where(i < n_lat_tiles, 0, 1), 0, 0)),
                  full(w_in), full(q_gain), full(k_gain), full(mla_q_gain), full(w_uq), full(mla_kv_gain),
                  full(w_ukv), row(LANES), row(LANES), row(LANES), row(LANES)],
        out_specs=[heads(GQA_HEADS, HEAD_DIM), heads(GQA_KV_HEADS, HEAD_DIM), heads_t(GQA_KV_HEADS, HEAD_DIM),
                   heads(MLA_HEADS, MLA_QK_PAD), heads(MLA_HEADS, MLA_QK_PAD), heads_t(MLA_HEADS, MLA_V),
                   row(LRU_WIDTH), row(LRU_WIDTH)],
        out_shape=[jax.ShapeDtypeStruct((GQA_HEADS, t, HEAD_DIM), BF16),
                   jax.ShapeDtypeStruct((GQA_KV_HEADS, t, HEAD_DIM), BF16),
                   jax.ShapeDtypeStruct((GQA_KV_HEADS, HEAD_DIM, t), BF16),
                   jax.ShapeDtypeStruct((MLA_HEADS, t, MLA_QK_PAD), BF16),
                   jax.ShapeDtypeStruct((MLA_HEADS, t, MLA_QK_PAD), BF16),
                   jax.ShapeDtypeStruct((MLA_HEADS, MLA_V, t), BF16),
                   jax.ShapeDtypeStruct((t, LRU_WIDTH), F32),
                   jax.ShapeDtypeStruct((t, LRU_WIDTH), F32)],
        compiler_params=_params(("arbitrary",)),
        name="inproj",
    )(x, mod, w_in, q_gain, k_gain, mla_q_gain, w_uq, mla_kv_gain, w_ukv, cg, sg, cm, sm)


ONES_ROWS = 16


def _scores_t(q, k_ref, start, size):
    k = k_ref[0, pl.ds(start, size), :]
    return lax.dot_general(k, q, (((1,), (1,)), ((), ())), preferred_element_type=F32)


def _softmax_step(st, vt_ref, start, size, carry, narrow_exp=False):
    m, acc = carry
    ones = jnp.where(lax.broadcasted_iota(jnp.int32, (ONES_ROWS, size), 0) == 0, 1.0, 0.0).astype(BF16)
    vt = jnp.concatenate([vt_ref[0, :, pl.ds(start, size)], ones], axis=0)
    m_new = jnp.maximum(m, jnp.max(st, axis=0, keepdims=True))
    alpha = jnp.exp2(m - m_new)
    if narrow_exp:
        pt = jnp.exp2((st - m_new).astype(BF16))
    else:
        pt = jnp.exp2(st - m_new).astype(BF16)
    acc = alpha * acc + jnp.dot(vt, pt, preferred_element_type=F32)
    return m_new, acc


def _attn_init(tq, dv):
    return jnp.full((1, tq), -1e30, F32), jnp.zeros((dv + ONES_ROWS, tq), F32)


def _attn_finish(carry, o_ref):
    _, acc = carry
    dv = acc.shape[0] - ONES_ROWS
    o_ref[...] = (acc[:dv, :] / acc[dv:dv + 1, :]).T.astype(o_ref.dtype)


def _attn_latent_kernel(n_main, tk, tail, unroll, narrow_exp, q_ref, k_ref, vt_ref, o_ref, s_ref):
    hp, tq = q_ref.shape[0], q_ref.shape[1]
    dv = vt_ref.shape[1]
    qs = [q_ref[h] for h in range(hp)]
    step = functools.partial(_softmax_step, narrow_exp=narrow_exp)

    def put(h, slot, st):
        s_ref[h, slot, :, 0:tq] = st

    def get(h, slot):
        return s_ref[h, slot, :, 0:tq]

    def body(jj, carries):
        c0 = pl.multiple_of(jj * (2 * tk), 2 * tk)
        carries = list(carries)
        for h in range(hp):
            put(h, 1, _scores_t(qs[h], k_ref, c0 + tk, tk))
            carries[h] = step(get(h, 0), vt_ref, c0, tk, carries[h])
        for h in range(hp):
            put(h, 0, _scores_t(qs[h], k_ref, c0 + 2 * tk, tk))
            carries[h] = step(get(h, 1), vt_ref, c0 + tk, tk, carries[h])
        return tuple(carries)

    carries = tuple(_attn_init(tq, dv) for _ in range(hp))
    for h in range(hp):
        put(h, 0, _scores_t(qs[h], k_ref, 0, tk))
    n_loop = (n_main - 1) // 2
    carries = list(lax.fori_loop(0, n_loop, body, carries, unroll=unroll))
    c0 = 2 * n_loop * tk
    two_left = n_main - 2 * n_loop == 2
    for h in range(hp):
        if two_left:
            put(h, 1, _scores_t(qs[h], k_ref, c0 + tk, tk))
        s_tail = _scores_t(qs[h], k_ref, n_main * tk, tail) if tail else None
        carries[h] = step(get(h, 0), vt_ref, c0, tk, carries[h])
        if two_left:
            carries[h] = step(get(h, 1), vt_ref, c0 + tk, tk, carries[h])
        if tail:
            carries[h] = step(s_tail, vt_ref, n_main * tk, tail, carries[h])
        _attn_finish(carries[h], o_ref.at[:, h * dv:(h + 1) * dv])


def _attn_context_kernel(q_ref, k_ref, vt_ref, o_ref):
    n = k_ref.shape[1]
    carry = _attn_init(q_ref.shape[1], vt_ref.shape[1])
    carry = _softmax_step(_scores_t(q_ref[0], k_ref, 0, n), vt_ref, 0, n, carry)
    _attn_finish(carry, o_ref)


def _attention(q, k, vt, n_lat, group, name, tk=ATTN_TK, unroll=2, hp=1, pad=0, tq=ATTN_TQ, narrow_exp=False):
    n_heads, t, dk = q.shape
    dv = vt.shape[1]
    n_ctx = t - n_lat
    tq = tq if n_lat % tq == 0 else ROW_TILE
    if t % tk == 0:
        n_main, tail = t // tk, 0
    else:
        tk = tk if n_lat % tk == 0 else ROW_TILE
        n_main, tail = n_lat // tk, n_ctx
    assert n_main * tk + tail == t and group % hp == 0 and n_heads % hp == 0
    lat = pl.pallas_call(
        functools.partial(_attn_latent_kernel, n_main, tk, tail, unroll, narrow_exp),
        grid=(n_heads // hp, n_lat // tq),
        in_specs=[pl.BlockSpec((hp, tq, dk), lambda h, i: (h, i, 0)),
                  pl.BlockSpec((1, t, dk), lambda h, i: ((h * hp) // group, 0, 0)),
                  pl.BlockSpec((1, dv, t), lambda h, i: ((h * hp) // group, 0, 0))],
        out_specs=pl.BlockSpec((tq, hp * dv), lambda h, i: (i, h)),
        out_shape=jax.ShapeDtypeStruct((n_lat, n_heads * dv), BF16),
        scratch_shapes=[pltpu.VMEM((hp, 2, tk, tq + pad), F32)],
        compiler_params=_params(("arbitrary", "arbitrary")),
        name=name + "_latent",
    )(q, k, vt)
    cblk = n_lat // n_ctx
    ctx = pl.pallas_call(
        _attn_context_kernel,
        grid=(n_heads,),
        in_specs=[pl.BlockSpec((1, n_ctx, dk), lambda h: (h, cblk, 0)),
                  pl.BlockSpec((1, n_ctx, dk), lambda h: (h // group, cblk, 0)),
                  pl.BlockSpec((1, dv, n_ctx), lambda h: (h // group, 0, cblk))],
        out_specs=pl.BlockSpec((n_ctx, dv), lambda h: (0, h)),
        out_shape=jax.ShapeDtypeStruct((n_ctx, n_heads * dv), BF16),
        compiler_params=_params(("arbitrary",)),
        name=name + "_context",
    )(q, k, vt)
    return lat, ctx


def _conv_kernel(n_lat, t_total, xp_ref, xc_ref, xn_ref, w_ref, b_ref, o_ref):
    i = pl.program_id(0)
    r = xc_ref.shape[0]
    halo = 8
    ext = jnp.concatenate([xp_ref[r - halo:r, :], xc_ref[...], xn_ref[0:halo, :]], axis=0)
    rows = i * r + lax.broadcasted_iota(jnp.int32, (r, 1), 0)
    in_ctx = rows >= n_lat
    seg_lo = jnp.where(in_ctx, n_lat, 0)
    seg_hi = jnp.where(in_ctx, t_total, n_lat)
    left = CONV_WIDTH // 2
    out = jnp.broadcast_to(b_ref[...], (r, xc_ref.shape[1]))
    for tap in range(CONV_WIDTH):
        off = tap - left
        shifted = ext if off == 0 else pltpu.roll(ext, (-off) % (r + 2 * halo), 0)
        xs = shifted[halo:halo + r, :]
        src = rows + off
        valid = jnp.logical_and(src >= seg_lo, src < seg_hi)
        out = out + jnp.where(valid, xs, 0.0) * w_ref[tap:tap + 1, :]
    o_ref[...] = out


def _conv(lx, conv_w, conv_b, n_lat):
    t, w = lx.shape
    r = ROW_TILE
    nt = t // r
    return pl.pallas_call(
        functools.partial(_conv_kernel, n_lat, t),
        grid=(nt,),
        in_specs=[pl.BlockSpec((r, w), lambda i: (jnp.maximum(i - 1, 0), 0)),
                  pl.BlockSpec((r, w), lambda i: (i, 0)),
                  pl.BlockSpec((r, w), lambda i: (jnp.minimum(i + 1, nt - 1), 0)),
                  pl.BlockSpec(conv_w.shape, lambda i: (0, 0)),
                  pl.BlockSpec((1, w), lambda i: (0, 0))],
        out_specs=pl.BlockSpec((r, w), lambda i: (i, 0)),
        out_shape=jax.ShapeDtypeStruct((t, w), F32),
        compiler_params=_params(("arbitrary",)),
        name="lru_conv",
    )(lx, lx, lx, conv_w, conv_b.reshape(1, w))


def _lru_coeffs(x, d, wr_ref, br_ref, wi_ref, bi_ref, lam_ref):
    xb = x.astype(BF16)
    r_parts, i_parts = [], []
    for n in range(LRU_BLOCKS):
        xs = xb[:, n * LRU_BLOCK_W:(n + 1) * LRU_BLOCK_W]
        r_parts.append(jnp.dot(xs, wr_ref[d, n], preferred_element_type=F32))
        i_parts.append(jnp.dot(xs, wi_ref[d, n], preferred_element_type=F32))
    r = jax.nn.sigmoid(jnp.concatenate(r_parts, axis=1) + br_ref[d:d + 1, :])
    ig = jax.nn.sigmoid(jnp.concatenate(i_parts, axis=1) + bi_ref[d:d + 1, :])
    neg_lam = -lam_ref[d:d + 1, :]
    softplus = jnp.maximum(neg_lam, 0.0) + jnp.log1p(jnp.exp(-jnp.abs(neg_lam)))
    log_a = -LRU_C * r * softplus
    a = jnp.exp(log_a)
    u = jnp.sqrt(-jnp.tanh(log_a) * (a * a + 1.0)) * ig * x
    return a, u


def _tile_scan(a, u, reverse):
    r = a.shape[0]
    rows = lax.broadcasted_iota(jnp.int32, (r, 1), 0)
    d = 1
    while d < r:
        if reverse:
            a_sh = pltpu.roll(a, r - d, 0)
            u_sh = pltpu.roll(u, r - d, 0)
            ok = rows < r - d
        else:
            a_sh = pltpu.roll(a, d, 0)
            u_sh = pltpu.roll(u, d, 0)
            ok = rows >= d
        u = jnp.where(ok, a * u_sh + u, u)
        a = jnp.where(ok, a * a_sh, a)
        d *= 2
    return a, u


def _scan_kernel(nc_tiles, xf_ref, xb_ref, wr_ref, br_ref, wi_ref, bi_ref, lam_ref, hf_ref, hb_ref, carry_ref):
    s = pl.program_id(0)
    r = xf_ref.shape[0]

    @pl.when(s == 0)
    def _():
        carry_ref[...] = jnp.zeros_like(carry_ref)

    del nc_tiles
    a, u = _lru_coeffs(xf_ref[...], 0, wr_ref, br_ref, wi_ref, bi_ref, lam_ref)
    a, u = _tile_scan(a, u, False)
    h = a * carry_ref[0:1, :] + u
    hf_ref[...] = h
    carry_ref[0:1, :] = h[r - 1:r, :]

    a, u = _lru_coeffs(xb_ref[...], 1, wr_ref, br_ref, wi_ref, bi_ref, lam_ref)
    a, u = _tile_scan(a, u, True)
    h = a * carry_ref[1:2, :] + u
    hb_ref[...] = h
    carry_ref[1:2, :] = h[0:1, :]


def _lru_scan(xc, w_r, b_r, w_i, b_i, lam, n_lat):
    t, w = xc.shape
    r = ROW_TILE
    nt, nl = t // r, n_lat // r
    nc = nt - nl
    fwd = lambda s: (jnp.where(s < nc, nl + s, s - nc), 0)
    bwd = lambda s: (jnp.where(s < nc, nt - 1 - s, nl - 1 - (s - nc)), 0)
    full = lambda a: pl.BlockSpec(a.shape, lambda s: (0,) * a.ndim)
    return pl.pallas_call(
        functools.partial(_scan_kernel, nc),
        grid=(nt,),
        in_specs=[pl.BlockSpec((r, w), fwd), pl.BlockSpec((r, w), bwd),
                  full(w_r), full(b_r), full(w_i), full(b_i), full(lam)],
        out_specs=[pl.BlockSpec((r, w), fwd), pl.BlockSpec((r, w), bwd)],
        out_shape=[jax.ShapeDtypeStruct((t, w), F32), jax.ShapeDtypeStruct((t, w), F32)],
        scratch_shapes=[pltpu.VMEM((8, w), F32)],
        compiler_params=_params(("arbitrary",)),
        name="lru_scan",
    )(xc, xc, w_r, b_r, w_i, b_i, lam)


def _layer_norm(y, g, b):
    mu = jnp.mean(y, axis=-1, keepdims=True)
    yc = y - mu
    var = jnp.mean(yc * yc, axis=-1, keepdims=True)
    return yc * lax.rsqrt(var + EPS) * g + b


def _gelu_tanh(x):
    return 0.5 * x * (1.0 + jnp.tanh(math.sqrt(2.0 / math.pi) * (x + 0.044715 * (x * x * x))))


def _outproj_kernel(alpha, with_router, n_lat_tiles, al_ref, ac_ref, ml_ref, mc_ref, hf_ref, hb_ref, lg_ref, x_ref,
                    w_ref, mod_ref, g_ref, b_ref, *rest):
    if with_router:
        router_ref, x1_o, fin_o, topi_o, gate_o = rest
    else:
        x1_o, fin_o = rest
    is_lat = pl.program_id(0) < n_lat_tiles
    a = jnp.where(is_lat, al_ref[...], ac_ref[...])
    m = jnp.where(is_lat, ml_ref[...], mc_ref[...])
    rec = _gelu_tanh(lg_ref[...]) * (hf_ref[...] + hb_ref[...])
    mix = jnp.dot(a, w_ref[0:D_GQA, :], preferred_element_type=F32)
    mix += jnp.dot(rec.astype(BF16), w_ref[D_GQA:D_GQA + LRU_WIDTH, :], preferred_element_type=F32)
    mix += jnp.dot(m, w_ref[D_GQA + LRU_WIDTH:, :], preferred_element_type=F32)
    y = alpha * x_ref[...] + mod_ref[0, 2:3, :] * mix
    x1 = _layer_norm(y, g_ref[...], b_ref[...])
    x1_o[...] = x1
    f_in = x1 * (1.0 + mod_ref[0, 4:5, :]) + mod_ref[0, 3:4, :]
    fin_o[...] = f_in
    if with_router:
        f_hi = f_in.astype(BF16)
        f_lo = (f_in - f_hi.astype(F32)).astype(BF16)
        r = router_ref[...]
        r_hi = r.astype(BF16)
        r_lo = (r - r_hi.astype(F32)).astype(BF16)
        logits = (jnp.dot(f_hi, r_hi, preferred_element_type=F32) + jnp.dot(f_hi, r_lo, preferred_element_type=F32)
                  + jnp.dot(f_lo, r_hi, preferred_element_type=F32))
        lane = lax.broadcasted_iota(jnp.int32, logits.shape, 1).astype(F32)
        neg = jnp.float32(-jnp.inf)
        lgt = jnp.where(lane < N_EXPERTS, logits, neg)
        v1 = jnp.max(lgt, axis=-1, keepdims=True)
        i1 = jnp.min(jnp.where(lgt == v1, lane, float(LANES)), axis=-1, keepdims=True)
        lgt2 = jnp.where(lane == i1, neg, lgt)
        v2 = jnp.max(lgt2, axis=-1, keepdims=True)
        i2 = jnp.min(jnp.where(lgt2 == v2, lane, float(LANES)), axis=-1, keepdims=True)
        e2 = jnp.exp(v2 - v1)
        den = 1.0 + e2
        topi_o[...] = jnp.where(lane == 0.0, i1, jnp.where(lane == 1.0, i2, 0.0)).astype(jnp.int32)
        gate_o[...] = jnp.where(lane == 0.0, 1.0 / den, jnp.where(lane == 1.0, e2 / den, 0.0))


def _outproj(a, m, hf, hb, lg, x, w_out, mod, ln_g, ln_b, alpha, n_lat, router):
    t, d = x.shape
    tm = ROW_TILE
    n_lat_tiles = n_lat // tm
    row = lambda w: pl.BlockSpec((tm, w), lambda i: (i, 0))
    lat_row = lambda w: pl.BlockSpec((tm, w), lambda i: (jnp.minimum(i, n_lat_tiles - 1), 0))
    ctx_row = lambda w: pl.BlockSpec((tm, w), lambda i: (jnp.maximum(i - n_lat_tiles, 0), 0))
    full = lambda arr: pl.BlockSpec(arr.shape, lambda i: (0,) * arr.ndim)
    in_specs = [lat_row(D_GQA), ctx_row(D_GQA), lat_row(D_MLA), ctx_row(D_MLA),
                row(LRU_WIDTH), row(LRU_WIDTH), row(LRU_WIDTH), row(d),
                full(w_out), pl.BlockSpec((1, 6, d), lambda i: (jnp.where(i < n_lat_tiles, 0, 1), 0, 0)),
                full(ln_g), full(ln_b)]
    out_specs = [row(d), row(d)]
    out_shape = [jax.ShapeDtypeStruct((t, d), F32), jax.ShapeDtypeStruct((t, d), F32)]
    args = [a[0], a[1], m[0], m[1], hf, hb, lg, x, w_out, mod, ln_g, ln_b]
    if router is not None:
        in_specs.append(full(router))
        args.append(router)
        out_specs += [row(LANES), row(LANES)]
        out_shape += [jax.ShapeDtypeStruct((t, LANES), jnp.int32), jax.ShapeDtypeStruct((t, LANES), F32)]
    return pl.pallas_call(
        functools.partial(_outproj_kernel, alpha, router is not None, n_lat_tiles),
        grid=(t // tm,),
        in_specs=in_specs, out_specs=out_specs, out_shape=out_shape,
        compiler_params=_params(("arbitrary",)),
        name="outproj",
    )(*args)


def _ffn_kernel(te_ref, nv_ref, x_ref, wg_ref, wu_ref, wd_ref, sc_ref, o_ref, acc_ref):
    del te_ref
    i, j = pl.program_id(0), pl.program_id(1)

    @pl.when(j == 0)
    def _():
        acc_ref[...] = jnp.zeros_like(acc_ref)

    @pl.when(i < nv_ref[0])
    def _():
        xb = x_ref[...].astype(BF16)
        g = jnp.dot(xb, wg_ref[0], preferred_element_type=F32)
        u = jnp.dot(xb, wu_ref[0], preferred_element_type=F32)
        hidden = (g * jax.nn.sigmoid(g)) * u
        acc_ref[...] += jnp.dot(hidden.astype(BF16), wd_ref[0], preferred_element_type=F32)

    @pl.when(j == pl.num_programs(1) - 1)
    def _():
        o_ref[...] = acc_ref[...] * sc_ref[...]


def _ffn(x, tile_expert, n_valid, w_gate, w_up, w_down, row_scale, tm):
    m_rows, d = x.shape
    f = w_gate.shape[2]
    tf = next((c for c in FFN_TF_CHOICES if f % c == 0), f)
    nj = f // tf
    col = lambda i, j, nv: jnp.where(i < nv[0], j, nj - 1)
    grid_spec = pltpu.PrefetchScalarGridSpec(
        num_scalar_prefetch=2,
        grid=(m_rows // tm, nj),
        in_specs=[pl.BlockSpec((tm, d), lambda i, j, te, nv: (i, 0)),
                  pl.BlockSpec((1, d, tf), lambda i, j, te, nv: (te[i], 0, col(i, j, nv))),
                  pl.BlockSpec((1, d, tf), lambda i, j, te, nv: (te[i], 0, col(i, j, nv))),
                  pl.BlockSpec((1, tf, d), lambda i, j, te, nv: (te[i], col(i, j, nv), 0)),
                  pl.BlockSpec((tm, 1), lambda i, j, te, nv: (i, 0))],
        out_specs=pl.BlockSpec((tm, d), lambda i, j, te, nv: (i, 0)),
        scratch_shapes=[pltpu.VMEM((tm, d), F32)],
    )
    return pl.pallas_call(
        _ffn_kernel,
        grid_spec=grid_spec,
        out_shape=jax.ShapeDtypeStruct((m_rows, d), F32),
        compiler_params=_params(("arbitrary", "arbitrary")),
        name="ffn",
    )(tile_expert, n_valid, x, w_gate, w_up, w_down, row_scale)


def _gather_kernel(idx_ref, src_ref, dst_ref, sem):
    base = pl.program_id(0) * GATHER_CHUNK

    def issue(r, carry):
        pltpu.make_async_copy(src_ref.at[pl.ds(idx_ref[base + r], 1)], dst_ref.at[pl.ds(r, 1)], sem).start()
        return carry

    lax.fori_loop(0, GATHER_CHUNK, issue, 0, unroll=8)
    pltpu.make_async_copy(src_ref.at[pl.ds(0, GATHER_CHUNK)], dst_ref, sem).wait()


def _row_gather(src, idx):
    m_rows = idx.shape[0]
    assert m_rows % GATHER_CHUNK == 0
    grid_spec = pltpu.PrefetchScalarGridSpec(
        num_scalar_prefetch=1,
        grid=(m_rows // GATHER_CHUNK,),
        in_specs=[pl.BlockSpec(memory_space=pl.ANY)],
        out_specs=pl.BlockSpec((GATHER_CHUNK, src.shape[1]), lambda c, idx: (c, 0)),
        scratch_shapes=[pltpu.SemaphoreType.DMA(())],
    )
    return pl.pallas_call(
        _gather_kernel,
        grid_spec=grid_spec,
        out_shape=jax.ShapeDtypeStruct((m_rows, src.shape[1]), src.dtype),
        compiler_params=_params(("arbitrary",)),
        name="row_gather",
    )(idx, src)


def _ln2_kernel(alpha, n_parts, x_ref, f_ref, mod_ref, g_ref, b_ref, o_ref):
    f = f_ref[0]
    for p in range(1, n_parts):
        f = f + f_ref[p]
    y = alpha * x_ref[...] + mod_ref[0, 5:6, :] * f
    o_ref[...] = _layer_norm(y, g_ref[...], b_ref[...])


def _ln2(x1, f_parts, mod, ln_g, ln_b, alpha, n_lat, n_rows):
    n_parts, _, d = f_parts.shape
    tm = ROW_TILE
    n_lat_tiles = n_lat // tm
    full = lambda arr: pl.BlockSpec(arr.shape, lambda i: (0,) * arr.ndim)
    return pl.pallas_call(
        functools.partial(_ln2_kernel, alpha, n_parts),
        grid=(n_rows // tm,),
        in_specs=[pl.BlockSpec((tm, d), lambda i: (i, 0)),
                  pl.BlockSpec((n_parts, tm, d), lambda i: (0, i, 0)),
                  pl.BlockSpec((1, 6, d), lambda i: (jnp.where(i < n_lat_tiles, 0, 1), 0, 0)),
                  full(ln_g), full(ln_b)],
        out_specs=pl.BlockSpec((tm, d), lambda i: (i, 0)),
        out_shape=jax.ShapeDtypeStruct((n_rows, d), F32),
        compiler_params=_params(("arbitrary",)),
        name="ln2",
    )(x1, f_parts, mod, ln_g, ln_b)


def _rope_tables(n_lat, n_ctx):
    rows = n_lat // GRID_W

    def angles(rot_dim):
        quarter = rot_dim // 4
        inv_freq = ROPE_THETA ** (-jnp.arange(quarter, dtype=F32) / quarter)
        row = jnp.repeat(jnp.arange(rows, dtype=F32), GRID_W)
        col = jnp.tile(jnp.arange(GRID_W, dtype=F32), rows)
        ang = jnp.concatenate([row[:, None] * inv_freq, col[:, None] * inv_freq], axis=-1)
        return jnp.cos(ang), jnp.sin(ang)

    def with_ctx(tab, fill):
        return jnp.concatenate([tab, jnp.full((n_ctx, tab.shape[1]), fill, F32)], axis=0)

    cos_g, sin_g = angles(HEAD_DIM)
    cg = with_ctx(jnp.concatenate([cos_g, cos_g], axis=1), 1.0)
    sg = with_ctx(jnp.concatenate([-sin_g, sin_g], axis=1), 0.0)
    cos_m, sin_m = angles(MLA_ROPE)
    zeros = jnp.zeros((n_lat, LANES - MLA_ROPE), F32)
    cm = with_ctx(jnp.concatenate([cos_m, cos_m, zeros], axis=1), 1.0)
    sm = with_ctx(jnp.concatenate([-sin_m, sin_m, zeros], axis=1), 0.0)
    return cg, sg, cm, sm


def _pad_w_uq(w_uq):
    depth, rank, _ = w_uq.shape
    w = w_uq.reshape(depth, rank, MLA_HEADS, MLA_NOPE + MLA_ROPE)
    w = jnp.pad(w, ((0, 0), (0, 0), (0, 0), (0, MLA_QK_PAD - MLA_NOPE - MLA_ROPE)))
    return w.reshape(depth, rank, MLA_HEADS * MLA_QK_PAD)


def _dispatch(top_i, gates, tm):
    t = top_i.shape[0]
    flat_e = top_i.T.reshape(-1)
    flat_g = gates.T.reshape(-1)
    onehot = (flat_e[:, None] == jnp.arange(N_EXPERTS, dtype=jnp.int32)[None, :]).astype(jnp.int32)
    rank = jnp.sum((jnp.cumsum(onehot, axis=0) - onehot) * onehot, axis=1)
    counts = jnp.sum(onehot, axis=0)
    ptiles = (counts + tm - 1) // tm
    tile_end = jnp.cumsum(ptiles)
    tile_start = tile_end - ptiles
    pos = (tile_start[flat_e] * tm + rank).astype(jnp.int32)
    n_tiles = (2 * t) // tm + N_EXPERTS
    m_pad = n_tiles * tm
    src = jnp.zeros((m_pad,), jnp.int32).at[pos].set(jnp.tile(jnp.arange(t, dtype=jnp.int32), 2))
    scale = jnp.zeros((m_pad,), F32).at[pos].set(flat_g)
    tile_ids = jnp.arange(n_tiles, dtype=jnp.int32)
    last_used = jnp.max(jnp.where(ptiles > 0, jnp.arange(N_EXPERTS, dtype=jnp.int32), 0))
    tile_expert = jnp.minimum(jnp.sum((tile_end[None, :] <= tile_ids[:, None]).astype(jnp.int32), axis=1),
                              last_used).astype(jnp.int32)
    n_valid = tile_end[-1:].astype(jnp.int32)
    return src, scale.reshape(m_pad, 1), pos, tile_expert, n_valid


def kernel(x, c, ctx, c_ctx, ada_w, ada_b, ln_g, ln_b, w_in, w_out, gqa_q_gain, gqa_k_gain, lru_conv_w, lru_conv_b, lru_w_r, lru_b_r, lru_w_i, lru_b_i, lru_lambda, mla_q_gain, mla_w_uq, mla_kv_gain, mla_w_ukv, ffn_w_gate, ffn_w_up, ffn_w_down, moe_router, moe_w_gate, moe_w_up, moe_w_down):
    batch, n_lat, d = x.shape
    n_ctx = ctx.shape[1]
    depth = ada_w.shape[0]
    assert batch == 1 and ctx.shape[0] == 1
    assert n_lat % ROW_TILE == 0 and n_ctx % ROW_TILE == 0 and n_lat % n_ctx == 0 and n_lat % GRID_W == 0
    t = n_lat + n_ctx
    alpha = (2 * depth) ** 0.25

    xs = jnp.concatenate([x[0], ctx[0]], axis=0)
    cond_t = jnp.stack([c[0], c_ctx], axis=1)
    mods = _modulation(cond_t, ada_w, ada_b).reshape(depth, 2, 6, d)
    tabs = _rope_tables(n_lat, n_ctx)

    w_in_b = jnp.pad(w_in, ((0, 0), (0, 0), (0, D_IN_PAD - D_IN))).astype(BF16)
    w_out_b = w_out.astype(BF16)
    w_uq_b = _pad_w_uq(mla_w_uq).astype(BF16)
    w_ukv_b = mla_w_ukv.astype(BF16)
    w_r_b = lru_w_r.astype(BF16)
    w_i_b = lru_w_i.astype(BF16)
    router_p = jnp.pad(moe_router, ((0, 0), (0, 0), (0, LANES - N_EXPERTS)))
    ffn_tm = FFN_TM if t % FFN_TM == 0 else ROW_TILE
    moe_tm = MOE_TM if (2 * t) % MOE_TM == 0 else ROW_TILE
    n_moe, d_exp = moe_w_gate.shape[0], moe_w_gate.shape[3]
    dense_w = (ffn_w_gate.astype(BF16), ffn_w_up.astype(BF16), ffn_w_down.astype(BF16))
    moe_w = (moe_w_gate.astype(BF16).reshape(n_moe * N_EXPERTS, d, d_exp),
             moe_w_up.astype(BF16).reshape(n_moe * N_EXPERTS, d, d_exp),
             moe_w_down.astype(BF16).reshape(n_moe * N_EXPERTS, d_exp, d))
    uni_tk = 640 if t % 640 == 0 else ATTN_TK
    big_tk = 1280 if t % 1280 == 0 else uni_tk
    gqa_cfg = [dict(hp=2), dict(hp=2, tk=uni_tk, unroll=3), dict(hp=2, tk=uni_tk, unroll=3),
               dict(hp=2, tk=uni_tk, unroll=3)]
    mla_cfg = [dict(tk=uni_tk, unroll=3), dict(tk=big_tk, unroll=2), dict(tk=uni_tk, unroll=3, narrow_exp=True),
               dict(tk=uni_tk, unroll=4)]

    for l in range(depth):
        last = l == depth - 1
        mod = mods[l]
        qg, kg, vg, qm, km, vm, lx, lg = _inproj(
            xs, mod, w_in_b[l], gqa_q_gain[l][None], gqa_k_gain[l][None], mla_q_gain[l][None], w_uq_b[l],
            mla_kv_gain[l][None], w_ukv_b[l], tabs, n_lat)
        a_out = _attention(qg, kg, vg, n_lat, GQA_HEADS // GQA_KV_HEADS, "gqa", **gqa_cfg[l % 4])
        m_out = _attention(qm, km, vm, n_lat, 1, "mla", **mla_cfg[l % 4])
        xc = _conv(lx, lru_conv_w[l], lru_conv_b[l], n_lat)
        hf, hb = _lru_scan(xc, w_r_b[l], lru_b_r[l], w_i_b[l], lru_b_i[l], lru_lambda[l], n_lat)
        moe = l % 2 == 1
        outs = _outproj(a_out, m_out, hf, hb, lg, xs, w_out_b[l], mod, ln_g[l, 0][None], ln_b[l, 0][None],
                        alpha, n_lat, router_p[l // 2] if moe else None)
        if moe:
            x1, f_in, topi, gate = outs
            src, scale, pos, tile_expert, n_valid = _dispatch(topi[:, :2], gate[:, :2], moe_tm)
            x_sorted = _row_gather(f_in, src)
            y = _ffn(x_sorted, tile_expert + (l // 2) * N_EXPERTS, n_valid, *moe_w, scale, moe_tm)
            f_parts = _row_gather(y, pos).reshape(2, t, d)
        else:
            x1, f_in = outs
            n_tiles = t // ffn_tm
            y = _ffn(f_in, jnp.full((n_tiles,), l // 2, jnp.int32), jnp.full((1,), n_tiles, jnp.int32),
                     *dense_w, jnp.ones((t, 1), F32), ffn_tm)
            f_parts = y[None]
        xs = _ln2(x1, f_parts, mod, ln_g[l, 1][None], ln_b[l, 1][None], alpha, n_lat, n_lat if last else t)
    return xs[None]
```

```python
import functools
import math

import jax
import jax.numpy as jnp
from jax import lax
from jax.experimental import pallas as pl
from jax.experimental.pallas import tpu as pltpu

GRID_W = 64
HEAD_DIM = 128
GQA_HEADS = 8
GQA_KV_HEADS = 2
LRU_WIDTH = 512
LRU_BLOCKS = 4
LRU_BLOCK_W = LRU_WIDTH // LRU_BLOCKS
CONV_WIDTH = 4
LRU_C = 8.0
MLA_HEADS = 4
MLA_Q_RANK = 384
MLA_KV_RANK = 256
MLA_NOPE = 128
MLA_ROPE = 64
MLA_V = 128
N_EXPERTS = 8
ROPE_THETA = 10000.0
EPS = 1e-6
LOG2E = 1.4426950408889634

LANES = 128
MLA_QK_PAD = 2 * LANES
OFF_GQ = 0
OFF_GK = OFF_GQ + GQA_HEADS * HEAD_DIM
OFF_GV = OFF_GK + GQA_KV_HEADS * HEAD_DIM
OFF_LX = OFF_GV + GQA_KV_HEADS * HEAD_DIM
OFF_LG = OFF_LX + LRU_WIDTH
OFF_CQ = OFF_LG + LRU_WIDTH
OFF_CKV = OFF_CQ + MLA_Q_RANK
OFF_KR = OFF_CKV + MLA_KV_RANK
D_IN = OFF_KR + MLA_ROPE
D_IN_PAD = OFF_KR + LANES
D_GQA = GQA_HEADS * HEAD_DIM
D_MLA = MLA_HEADS * MLA_V

ROW_TILE = 256
ATTN_TQ = 512
ATTN_TK = 512
GQA_TK_CHOICES = (640,)
MLA_TK_CHOICES = (1280, 640)
FFN_TM = 640
MOE_TM = 512
FFN_TF_CHOICES = (1024, 512)
GATHER_CHUNK = 512
VMEM_LIMIT = 56 * 1024 * 1024

BF16 = jnp.bfloat16
F32 = jnp.float32


def _params(sem, vmem=VMEM_LIMIT):
    return pltpu.CompilerParams(dimension_semantics=sem, vmem_limit_bytes=vmem)


def _mod_kernel(cond_ref, w_ref, b_ref, o_ref):
    cc = cond_ref[...]
    s = cc * jax.nn.sigmoid(cc)
    w = w_ref[0]
    m0 = jnp.sum(w * s[:, 0:1], axis=0, keepdims=True)
    m1 = jnp.sum(w * s[:, 1:2], axis=0, keepdims=True)
    o_ref[0] = jnp.concatenate([m0, m1], axis=0) + b_ref[0]


def _modulation(cond_t, ada_w, ada_b):
    depth, d, n = ada_w.shape
    tn = 1024 if n % 1024 == 0 else n
    return pl.pallas_call(
        _mod_kernel,
        grid=(depth, n // tn),
        in_specs=[pl.BlockSpec((d, 2), lambda l, j: (0, 0)),
                  pl.BlockSpec((1, d, tn), lambda l, j: (l, 0, j)),
                  pl.BlockSpec((1, 1, tn), lambda l, j: (l, 0, j))],
        out_specs=pl.BlockSpec((1, 2, tn), lambda l, j: (l, 0, j)),
        out_shape=jax.ShapeDtypeStruct((depth, 2, n), F32),
        compiler_params=_params(("arbitrary", "arbitrary")),
        name="modulation",
    )(cond_t, ada_w, ada_b.reshape(depth, 1, n))


def _rms(x, g):
    return x * lax.rsqrt(jnp.mean(x * x, axis=-1, keepdims=True) + EPS) * g


def _inproj_kernel(x_ref, mod_ref, w_ref, qg_ref, kg_ref, mqg_ref, wuq_ref, mkg_ref, wukv_ref,
                   cg_ref, sg_ref, cm_ref, sm_ref,
                   qg_o, kg_o, vg_o, qm_o, km_o, vm_o, lx_o, lg_o):
    shift = mod_ref[0, 0:1, :]
    scale = mod_ref[0, 1:2, :]
    h = x_ref[...] * (1.0 + scale) + shift
    proj = jnp.dot(h.astype(BF16), w_ref[...], preferred_element_type=F32)

    cg = cg_ref[...]
    sg = sg_ref[...]

    def rope_gqa(v):
        return v * cg + pltpu.roll(v, HEAD_DIM // 2, 1) * sg

    q_scale = HEAD_DIM ** -0.5 * LOG2E
    for hh in range(GQA_HEADS):
        q = _rms(proj[:, OFF_GQ + hh * HEAD_DIM: OFF_GQ + (hh + 1) * HEAD_DIM], qg_ref[...])
        qg_o[hh] = (rope_gqa(q) * q_scale).astype(BF16)
    for hh in range(GQA_KV_HEADS):
        k = _rms(proj[:, OFF_GK + hh * HEAD_DIM: OFF_GK + (hh + 1) * HEAD_DIM], kg_ref[...])
        kg_o[hh] = rope_gqa(k).astype(BF16)
        vg_o[hh] = proj[:, OFF_GV + hh * HEAD_DIM: OFF_GV + (hh + 1) * HEAD_DIM].T.astype(BF16)

    lx_o[...] = proj[:, OFF_LX:OFF_LX + LRU_WIDTH]
    lg_o[...] = proj[:, OFF_LG:OFF_LG + LRU_WIDTH]

    cq = _rms(proj[:, OFF_CQ:OFF_CQ + MLA_Q_RANK], mqg_ref[...])
    qm = jnp.dot(cq.astype(BF16), wuq_ref[...], preferred_element_type=F32)
    ckv = _rms(proj[:, OFF_CKV:OFF_CKV + MLA_KV_RANK], mkg_ref[...])
    kv = jnp.dot(ckv.astype(BF16), wukv_ref[...], preferred_element_type=F32)

    cm = cm_ref[...]
    sm = sm_ref[...]

    def rope_mla(v):
        swapped = pltpu.roll(v, MLA_ROPE // 2, 1) + pltpu.roll(v, LANES - MLA_ROPE // 2, 1)
        return v * cm + swapped * sm

    kr = rope_mla(proj[:, OFF_KR:OFF_KR + LANES])
    m_scale = (MLA_NOPE + MLA_ROPE) ** -0.5 * LOG2E
    for hh in range(MLA_HEADS):
        base = hh * MLA_QK_PAD
        qn = qm[:, base:base + MLA_NOPE]
        qr = rope_mla(qm[:, base + MLA_NOPE:base + MLA_QK_PAD])
        qm_o[hh] = (jnp.concatenate([qn, qr], axis=1) * m_scale).astype(BF16)
        kvb = hh * (MLA_NOPE + MLA_V)
        km_o[hh] = jnp.concatenate([kv[:, kvb:kvb + MLA_NOPE], kr], axis=1).astype(BF16)
        vm_o[hh] = kv[:, kvb + MLA_NOPE:kvb + MLA_NOPE + MLA_V].T.astype(BF16)


def _inproj(x, mod, w_in, q_gain, k_gain, mla_q_gain, w_uq, mla_kv_gain, w_ukv, tabs, n_lat):
    t, d = x.shape
    tm = ROW_TILE
    n_lat_tiles = n_lat // tm
    full = lambda a: pl.BlockSpec(a.shape, lambda i: (0,) * a.ndim)
    row = lambda w: pl.BlockSpec((tm, w), lambda i: (i, 0))
    heads = lambda n, w: pl.BlockSpec((n, tm, w), lambda i: (0, i, 0))
    heads_t = lambda n, w: pl.BlockSpec((n, w, tm), lambda i: (0, 0, i))
    cg, sg, cm, sm = tabs
    return pl.pallas_call(
        _inproj_kernel,
        grid=(t // tm,),
        in_specs=[row(d),
                  pl.BlockSpec((1, 6, d), lambda i: (jnp.where(i < n_lat_tiles, 0, 1), 0, 0)),
                  full(w_in), full(q_gain), full(k_gain), full(mla_q_gain), full(w_uq), full(mla_kv_gain),
                  full(w_ukv), row(LANES), row(LANES), row(LANES), row(LANES)],
        out_specs=[heads(GQA_HEADS, HEAD_DIM), heads(GQA_KV_HEADS, HEAD_DIM), heads_t(GQA_KV_HEADS, HEAD_DIM),
                   heads(MLA_HEADS, MLA_QK_PAD), heads(MLA_HEADS, MLA_QK_PAD), heads_t(MLA_HEADS, MLA_V),
                   row(LRU_WIDTH), row(LRU_WIDTH)],
        out_shape=[jax.ShapeDtypeStruct((GQA_HEADS, t, HEAD_DIM), BF16),
                   jax.ShapeDtypeStruct((GQA_KV_HEADS, t, HEAD_DIM), BF16),
                   jax.ShapeDtypeStruct((GQA_KV_HEADS, HEAD_DIM, t), BF16),
                   jax.ShapeDtypeStruct((MLA_HEADS, t, MLA_QK_PAD), BF16),
                   jax.ShapeDtypeStruct((MLA_HEADS, t, MLA_QK_PAD), BF16),
                   jax.ShapeDtypeStruct((MLA_HEADS, MLA_V, t), BF16),
                   jax.ShapeDtypeStruct((t, LRU_WIDTH), F32),
                   jax.ShapeDtypeStruct((t, LRU_WIDTH), F32)],
        compiler_params=_params(("arbitrary",)),
        name="inproj",
    )(x, mod, w_in, q_gain, k_gain, mla_q_gain, w_uq, mla_kv_gain, w_ukv, cg, sg, cm, sm)


ONES_ROWS = 16


def _scores_t(q, k_ref, start, size):
    k = k_ref[0, pl.ds(start, size), :]
    return lax.dot_general(k, q, (((1,), (1,)), ((), ())), preferred_element_type=F32)


def _softmax_step(st, vt_ref, start, size, carry):
    m, acc = carry
    ones = jnp.where(lax.broadcasted_iota(jnp.int32, (ONES_ROWS, size), 0) == 0, 1.0, 0.0).astype(BF16)
    vt = jnp.concatenate([vt_ref[0, :, pl.ds(start, size)], ones], axis=0)
    m_new = jnp.maximum(m, jnp.max(st, axis=0, keepdims=True))
    alpha = jnp.exp2(m - m_new)
    pt = jnp.exp2(st - m_new).astype(BF16)
    acc = alpha * acc + jnp.dot(vt, pt, preferred_element_type=F32)
    return m_new, acc


def _attn_init(tq, dv):
    return jnp.full((1, tq), -1e30, F32), jnp.zeros((dv + ONES_ROWS, tq), F32)


def _attn_finish(carry, o_ref):
    _, acc = carry
    dv = acc.shape[0] - ONES_ROWS
    o_ref[...] = (acc[:dv, :] / acc[dv:dv + 1, :]).T.astype(o_ref.dtype)


def _attn_latent_kernel(n_main, tk, tail, unroll, q_ref, k_ref, vt_ref, o_ref, s_ref):
    hp, tq = q_ref.shape[0], q_ref.shape[1]
    dv = vt_ref.shape[1]
    qs = [q_ref[h] for h in range(hp)]
    step = _softmax_step

    def put(h, slot, st):
        s_ref[h, slot] = st

    def get(h, slot):
        return s_ref[h, slot]

    def body(jj, carries):
        c0 = pl.multiple_of(jj * (2 * tk), 2 * tk)
        carries = list(carries)
        for h in range(hp):
            put(h, 1, _scores_t(qs[h], k_ref, c0 + tk, tk))
            carries[h] = step(get(h, 0), vt_ref, c0, tk, carries[h])
        for h in range(hp):
            put(h, 0, _scores_t(qs[h], k_ref, c0 + 2 * tk, tk))
            carries[h] = step(get(h, 1), vt_ref, c0 + tk, tk, carries[h])
        return tuple(carries)

    carries = tuple(_attn_init(tq, dv) for _ in range(hp))
    for h in range(hp):
        put(h, 0, _scores_t(qs[h], k_ref, 0, tk))
    n_loop = (n_main - 1) // 2
    carries = list(lax.fori_loop(0, n_loop, body, carries, unroll=unroll))
    c0 = 2 * n_loop * tk
    two_left = n_main - 2 * n_loop == 2
    for h in range(hp):
        if two_left:
            put(h, 1, _scores_t(qs[h], k_ref, c0 + tk, tk))
        s_tail = _scores_t(qs[h], k_ref, n_main * tk, tail) if tail else None
        carries[h] = step(get(h, 0), vt_ref, c0, tk, carries[h])
        if two_left:
            carries[h] = step(get(h, 1), vt_ref, c0 + tk, tk, carries[h])
        if tail:
            carries[h] = step(s_tail, vt_ref, n_main * tk, tail, carries[h])
        _attn_finish(carries[h], o_ref.at[:, h * dv:(h + 1) * dv])


def _attn_context_kernel(q_ref, k_ref, vt_ref, o_ref):
    n = k_ref.shape[1]
    carry = _attn_init(q_ref.shape[1], vt_ref.shape[1])
    carry = _softmax_step(_scores_t(q_ref[0], k_ref, 0, n), vt_ref, 0, n, carry)
    _attn_finish(carry, o_ref)


def _attention(q, k, vt, n_lat, group, name, tk, unroll, hp):
    n_heads, t, dk = q.shape
    dv = vt.shape[1]
    n_ctx = t - n_lat
    tq = ATTN_TQ if n_lat % ATTN_TQ == 0 else ROW_TILE
    if t % tk == 0:
        n_main, tail = t // tk, 0
    else:
        tk = tk if n_lat % tk == 0 else ROW_TILE
        n_main, tail = n_lat // tk, n_ctx
    assert n_main * tk + tail == t and group % hp == 0 and n_heads % hp == 0
    lat = pl.pallas_call(
        functools.partial(_attn_latent_kernel, n_main, tk, tail, unroll),
        grid=(n_heads // hp, n_lat // tq),
        in_specs=[pl.BlockSpec((hp, tq, dk), lambda h, i: (h, i, 0)),
                  pl.BlockSpec((1, t, dk), lambda h, i: ((h * hp) // group, 0, 0)),
                  pl.BlockSpec((1, dv, t), lambda h, i: ((h * hp) // group, 0, 0))],
        out_specs=pl.BlockSpec((tq, hp * dv), lambda h, i: (i, h)),
        out_shape=jax.ShapeDtypeStruct((n_lat, n_heads * dv), BF16),
        scratch_shapes=[pltpu.VMEM((hp, 2, tk, tq), F32)],
        compiler_params=_params(("arbitrary", "arbitrary")),
        name=name + "_latent",
    )(q, k, vt)
    cblk = n_lat // n_ctx
    ctx = pl.pallas_call(
        _attn_context_kernel,
        grid=(n_heads,),
        in_specs=[pl.BlockSpec((1, n_ctx, dk), lambda h: (h, cblk, 0)),
                  pl.BlockSpec((1, n_ctx, dk), lambda h: (h // group, cblk, 0)),
                  pl.BlockSpec((1, dv, n_ctx), lambda h: (h // group, 0, cblk))],
        out_specs=pl.BlockSpec((n_ctx, dv), lambda h: (0, h)),
        out_shape=jax.ShapeDtypeStruct((n_ctx, n_heads * dv), BF16),
        compiler_params=_params(("arbitrary",)),
        name=name + "_context",
    )(q, k, vt)
    return lat, ctx


def _conv_kernel(n_lat, t_total, xp_ref, xc_ref, xn_ref, w_ref, b_ref, o_ref):
    i = pl.program_id(0)
    r = xc_ref.shape[0]
    halo = 8
    ext = jnp.concatenate([xp_ref[r - halo:r, :], xc_ref[...], xn_ref[0:halo, :]], axis=0)
    rows = i * r + lax.broadcasted_iota(jnp.int32, (r, 1), 0)
    in_ctx = rows >= n_lat
    seg_lo = jnp.where(in_ctx, n_lat, 0)
    seg_hi = jnp.where(in_ctx, t_total, n_lat)
    left = CONV_WIDTH // 2
    out = jnp.broadcast_to(b_ref[...], (r, xc_ref.shape[1]))
    for tap in range(CONV_WIDTH):
        off = tap - left
        shifted = ext if off == 0 else pltpu.roll(ext, (-off) % (r + 2 * halo), 0)
        xs = shifted[halo:halo + r, :]
        src = rows + off
        valid = jnp.logical_and(src >= seg_lo, src < seg_hi)
        out = out + jnp.where(valid, xs, 0.0) * w_ref[tap:tap + 1, :]
    o_ref[...] = out


def _conv(lx, conv_w, conv_b, n_lat):
    t, w = lx.shape
    r = ROW_TILE
    nt = t // r
    return pl.pallas_call(
        functools.partial(_conv_kernel, n_lat, t),
        grid=(nt,),
        in_specs=[pl.BlockSpec((r, w), lambda i: (jnp.maximum(i - 1, 0), 0)),
                  pl.BlockSpec((r, w), lambda i: (i, 0)),
                  pl.BlockSpec((r, w), lambda i: (jnp.minimum(i + 1, nt - 1), 0)),
                  pl.BlockSpec(conv_w.shape, lambda i: (0, 0)),
                  pl.BlockSpec((1, w), lambda i: (0, 0))],
        out_specs=pl.BlockSpec((r, w), lambda i: (i, 0)),
        out_shape=jax.ShapeDtypeStruct((t, w), F32),
        compiler_params=_params(("arbitrary",)),
        name="lru_conv",
    )(lx, lx, lx, conv_w, conv_b.reshape(1, w))


def _lru_coeffs(x, d, wr_ref, br_ref, wi_ref, bi_ref, lam_ref):
    xb = x.astype(BF16)
    r_parts, i_parts = [], []
    for n in range(LRU_BLOCKS):
        xs = xb[:, n * LRU_BLOCK_W:(n + 1) * LRU_BLOCK_W]
        r_parts.append(jnp.dot(xs, wr_ref[d, n], preferred_element_type=F32))
        i_parts.append(jnp.dot(xs, wi_ref[d, n], preferred_element_type=F32))
    r = jax.nn.sigmoid(jnp.concatenate(r_parts, axis=1) + br_ref[d:d + 1, :])
    ig = jax.nn.sigmoid(jnp.concatenate(i_parts, axis=1) + bi_ref[d:d + 1, :])
    neg_lam = -lam_ref[d:d + 1, :]
    softplus = jnp.maximum(neg_lam, 0.0) + jnp.log1p(jnp.exp(-jnp.abs(neg_lam)))
    log_a = -LRU_C * r * softplus
    a = jnp.exp(log_a)
    u = jnp.sqrt(-jnp.tanh(log_a) * (a * a + 1.0)) * ig * x
    return a, u


def _tile_scan(a, u, reverse):
    r = a.shape[0]
    rows = lax.broadcasted_iota(jnp.int32, (r, 1), 0)
    d = 1
    while d < r:
        if reverse:
            a_sh = pltpu.roll(a, r - d, 0)
            u_sh = pltpu.roll(u, r - d, 0)
            ok = rows < r - d
        else:
            a_sh = pltpu.roll(a, d, 0)
            u_sh = pltpu.roll(u, d, 0)
            ok = rows >= d
        u = jnp.where(ok, a * u_sh + u, u)
        a = jnp.where(ok, a * a_sh, a)
        d *= 2
    return a, u


def _scan_kernel(nc_tiles, xf_ref, xb_ref, wr_ref, br_ref, wi_ref, bi_ref, lam_ref, hf_ref, hb_ref, carry_ref):
    s = pl.program_id(0)
    r = xf_ref.shape[0]

    @pl.when(s == 0)
    def _():
        carry_ref[...] = jnp.zeros_like(carry_ref)

    del nc_tiles
    a, u = _lru_coeffs(xf_ref[...], 0, wr_ref, br_ref, wi_ref, bi_ref, lam_ref)
    a, u = _tile_scan(a, u, False)
    h = a * carry_ref[0:1, :] + u
    hf_ref[...] = h
    carry_ref[0:1, :] = h[r - 1:r, :]

    a, u = _lru_coeffs(xb_ref[...], 1, wr_ref, br_ref, wi_ref, bi_ref, lam_ref)
    a, u = _tile_scan(a, u, True)
    h = a * carry_ref[1:2, :] + u
    hb_ref[...] = h
    carry_ref[1:2, :] = h[0:1, :]


def _lru_scan(xc, w_r, b_r, w_i, b_i, lam, n_lat):
    t, w = xc.shape
    r = ROW_TILE
    nt, nl = t // r, n_lat // r
    nc = nt - nl
    fwd = lambda s: (jnp.where(s < nc, nl + s, s - nc), 0)
    bwd = lambda s: (jnp.where(s < nc, nt - 1 - s, nl - 1 - (s - nc)), 0)
    full = lambda a: pl.BlockSpec(a.shape, lambda s: (0,) * a.ndim)
    return pl.pallas_call(
        functools.partial(_scan_kernel, nc),
        grid=(nt,),
        in_specs=[pl.BlockSpec((r, w), fwd), pl.BlockSpec((r, w), bwd),
                  full(w_r), full(b_r), full(w_i), full(b_i), full(lam)],
        out_specs=[pl.BlockSpec((r, w), fwd), pl.BlockSpec((r, w), bwd)],
        out_shape=[jax.ShapeDtypeStruct((t, w), F32), jax.ShapeDtypeStruct((t, w), F32)],
        scratch_shapes=[pltpu.VMEM((8, w), F32)],
        compiler_params=_params(("arbitrary",)),
        name="lru_scan",
    )(xc, xc, w_r, b_r, w_i, b_i, lam)


def _layer_norm(y, g, b):
    mu = jnp.mean(y, axis=-1, keepdims=True)
    yc = y - mu
    var = jnp.mean(yc * yc, axis=-1, keepdims=True)
    return yc * lax.rsqrt(var + EPS) * g + b


def _gelu_tanh(x):
    return 0.5 * x * (1.0 + jnp.tanh(math.sqrt(2.0 / math.pi) * (x + 0.044715 * (x * x * x))))


def _outproj_kernel(alpha, with_router, n_lat_tiles, al_ref, ac_ref, ml_ref, mc_ref, hf_ref, hb_ref, lg_ref, x_ref,
                    w_ref, mod_ref, g_ref, b_ref, *rest):
    if with_router:
        router_ref, x1_o, fin_o, topi_o, gate_o = rest
    else:
        x1_o, fin_o = rest
    is_lat = pl.program_id(0) < n_lat_tiles
    a = jnp.where(is_lat, al_ref[...], ac_ref[...])
    m = jnp.where(is_lat, ml_ref[...], mc_ref[...])
    rec = _gelu_tanh(lg_ref[...]) * (hf_ref[...] + hb_ref[...])
    mix = jnp.dot(a, w_ref[0:D_GQA, :], preferred_element_type=F32)
    mix += jnp.dot(rec.astype(BF16), w_ref[D_GQA:D_GQA + LRU_WIDTH, :], preferred_element_type=F32)
    mix += jnp.dot(m, w_ref[D_GQA + LRU_WIDTH:, :], preferred_element_type=F32)
    y = alpha * x_ref[...] + mod_ref[0, 2:3, :] * mix
    x1 = _layer_norm(y, g_ref[...], b_ref[...])
    x1_o[...] = x1
    f_in = x1 * (1.0 + mod_ref[0, 4:5, :]) + mod_ref[0, 3:4, :]
    fin_o[...] = f_in
    if with_router:
        f_hi = f_in.astype(BF16)
        f_lo = (f_in - f_hi.astype(F32)).astype(BF16)
        r = router_ref[...]
        r_hi = r.astype(BF16)
        r_lo = (r - r_hi.astype(F32)).astype(BF16)
        logits = (jnp.dot(f_hi, r_hi, preferred_element_type=F32) + jnp.dot(f_hi, r_lo, preferred_element_type=F32)
                  + jnp.dot(f_lo, r_hi, preferred_element_type=F32))
        lane = lax.broadcasted_iota(jnp.int32, logits.shape, 1).astype(F32)
        neg = jnp.float32(-jnp.inf)
        lgt = jnp.where(lane < N_EXPERTS, logits, neg)
        v1 = jnp.max(lgt, axis=-1, keepdims=True)
        i1 = jnp.min(jnp.where(lgt == v1, lane, float(LANES)), axis=-1, keepdims=True)
        lgt2 = jnp.where(lane == i1, neg, lgt)
        v2 = jnp.max(lgt2, axis=-1, keepdims=True)
        i2 = jnp.min(jnp.where(lgt2 == v2, lane, float(LANES)), axis=-1, keepdims=True)
        e2 = jnp.exp(v2 - v1)
        den = 1.0 + e2
        topi_o[...] = jnp.where(lane == 0.0, i1, jnp.where(lane == 1.0, i2, 0.0)).astype(jnp.int32)
        gate_o[...] = jnp.where(lane == 0.0, 1.0 / den, jnp.where(lane == 1.0, e2 / den, 0.0))


def _outproj(a, m, hf, hb, lg, x, w_out, mod, ln_g, ln_b, alpha, n_lat, router):
    t, d = x.shape
    tm = ROW_TILE
    n_lat_tiles = n_lat // tm
    row = lambda w: pl.BlockSpec((tm, w), lambda i: (i, 0))
    lat_row = lambda w: pl.BlockSpec((tm, w), lambda i: (jnp.minimum(i, n_lat_tiles - 1), 0))
    ctx_row = lambda w: pl.BlockSpec((tm, w), lambda i: (jnp.maximum(i - n_lat_tiles, 0), 0))
    full = lambda arr: pl.BlockSpec(arr.shape, lambda i: (0,) * arr.ndim)
    in_specs = [lat_row(D_GQA), ctx_row(D_GQA), lat_row(D_MLA), ctx_row(D_MLA),
                row(LRU_WIDTH), row(LRU_WIDTH), row(LRU_WIDTH), row(d),
                full(w_out), pl.BlockSpec((1, 6, d), lambda i: (jnp.where(i < n_lat_tiles, 0, 1), 0, 0)),
                full(ln_g), full(ln_b)]
    out_specs = [row(d), row(d)]
    out_shape = [jax.ShapeDtypeStruct((t, d), F32), jax.ShapeDtypeStruct((t, d), F32)]
    args = [a[0], a[1], m[0], m[1], hf, hb, lg, x, w_out, mod, ln_g, ln_b]
    if router is not None:
        in_specs.append(full(router))
        args.append(router)
        out_specs += [row(LANES), row(LANES)]
        out_shape += [jax.ShapeDtypeStruct((t, LANES), jnp.int32), jax.ShapeDtypeStruct((t, LANES), F32)]
    return pl.pallas_call(
        functools.partial(_outproj_kernel, alpha, router is not None, n_lat_tiles),
        grid=(t // tm,),
        in_specs=in_specs, out_specs=out_specs, out_shape=out_shape,
        compiler_params=_params(("arbitrary",)),
        name="outproj",
    )(*args)


def _ffn_kernel(te_ref, nv_ref, x_ref, wg_ref, wu_ref, wd_ref, sc_ref, o_ref, acc_ref):
    del te_ref
    i, j = pl.program_id(0), pl.program_id(1)

    @pl.when(j == 0)
    def _():
        acc_ref[...] = jnp.zeros_like(acc_ref)

    @pl.when(i < nv_ref[0])
    def _():
        xb = x_ref[...].astype(BF16)
        g = jnp.dot(xb, wg_ref[0], preferred_element_type=F32)
        u = jnp.dot(xb, wu_ref[0], preferred_element_type=F32)
        hidden = (g * jax.nn.sigmoid(g)) * u
        acc_ref[...] += jnp.dot(hidden.astype(BF16), wd_ref[0], preferred_element_type=F32)

    @pl.when(j == pl.num_programs(1) - 1)
    def _():
        o_ref[...] = acc_ref[...] * sc_ref[...]


def _ffn(x, tile_expert, n_valid, w_gate, w_up, w_down, row_scale, tm):
    m_rows, d = x.shape
    f = w_gate.shape[2]
    tf = next((c for c in FFN_TF_CHOICES if f % c == 0), f)
    nj = f // tf
    col = lambda i, j, nv: jnp.where(i < nv[0], j, nj - 1)
    grid_spec = pltpu.PrefetchScalarGridSpec(
        num_scalar_prefetch=2,
        grid=(m_rows // tm, nj),
        in_specs=[pl.BlockSpec((tm, d), lambda i, j, te, nv: (i, 0)),
                  pl.BlockSpec((1, d, tf), lambda i, j, te, nv: (te[i], 0, col(i, j, nv))),
                  pl.BlockSpec((1, d, tf), lambda i, j, te, nv: (te[i], 0, col(i, j, nv))),
                  pl.BlockSpec((1, tf, d), lambda i, j, te, nv: (te[i], col(i, j, nv), 0)),
                  pl.BlockSpec((tm, 1), lambda i, j, te, nv: (i, 0))],
        out_specs=pl.BlockSpec((tm, d), lambda i, j, te, nv: (i, 0)),
        scratch_shapes=[pltpu.VMEM((tm, d), F32)],
    )
    return pl.pallas_call(
        _ffn_kernel,
        grid_spec=grid_spec,
        out_shape=jax.ShapeDtypeStruct((m_rows, d), F32),
        compiler_params=_params(("arbitrary", "arbitrary")),
        name="ffn",
    )(tile_expert, n_valid, x, w_gate, w_up, w_down, row_scale)


def _gather_kernel(idx_ref, src_ref, dst_ref, sem):
    base = pl.program_id(0) * GATHER_CHUNK

    def issue(r, carry):
        pltpu.make_async_copy(src_ref.at[pl.ds(idx_ref[base + r], 1)], dst_ref.at[pl.ds(r, 1)], sem).start()
        return carry

    lax.fori_loop(0, GATHER_CHUNK, issue, 0, unroll=8)
    pltpu.make_async_copy(src_ref.at[pl.ds(0, GATHER_CHUNK)], dst_ref, sem).wait()


def _row_gather(src, idx):
    m_rows = idx.shape[0]
    assert m_rows % GATHER_CHUNK == 0
    grid_spec = pltpu.PrefetchScalarGridSpec(
        num_scalar_prefetch=1,
        grid=(m_rows // GATHER_CHUNK,),
        in_specs=[pl.BlockSpec(memory_space=pl.ANY)],
        out_specs=pl.BlockSpec((GATHER_CHUNK, src.shape[1]), lambda c, idx: (c, 0)),
        scratch_shapes=[pltpu.SemaphoreType.DMA(())],
    )
    return pl.pallas_call(
        _gather_kernel,
        grid_spec=grid_spec,
        out_shape=jax.ShapeDtypeStruct((m_rows, src.shape[1]), src.dtype),
        compiler_params=_params(("arbitrary",)),
        name="row_gather",
    )(idx, src)


def _ln2_kernel(alpha, n_parts, x_ref, f_ref, mod_ref, g_ref, b_ref, o_ref):
    f = f_ref[0]
    for p in range(1, n_parts):
        f = f + f_ref[p]
    y = alpha * x_ref[...] + mod_ref[0, 5:6, :] * f
    o_ref[...] = _layer_norm(y, g_ref[...], b_ref[...])


def _ln2(x1, f_parts, mod, ln_g, ln_b, alpha, n_lat, n_rows):
    n_parts, _, d = f_parts.shape
    tm = ROW_TILE
    n_lat_tiles = n_lat // tm
    full = lambda arr: pl.BlockSpec(arr.shape, lambda i: (0,) * arr.ndim)
    return pl.pallas_call(
        functools.partial(_ln2_kernel, alpha, n_parts),
        grid=(n_rows // tm,),
        in_specs=[pl.BlockSpec((tm, d), lambda i: (i, 0)),
                  pl.BlockSpec((n_parts, tm, d), lambda i: (0, i, 0)),
                  pl.BlockSpec((1, 6, d), lambda i: (jnp.where(i < n_lat_tiles, 0, 1), 0, 0)),
                  full(ln_g), full(ln_b)],
        out_specs=pl.BlockSpec((tm, d), lambda i: (i, 0)),
        out_shape=jax.ShapeDtypeStruct((n_rows, d), F32),
        compiler_params=_params(("arbitrary",)),
        name="ln2",
    )(x1, f_parts, mod, ln_g, ln_b)


def _rope_tables(n_lat, n_ctx):
    rows = n_lat // GRID_W

    def angles(rot_dim):
        quarter = rot_dim // 4
        inv_freq = ROPE_THETA ** (-jnp.arange(quarter, dtype=F32) / quarter)
        row = jnp.repeat(jnp.arange(rows, dtype=F32), GRID_W)
        col = jnp.tile(jnp.arange(GRID_W, dtype=F32), rows)
        ang = jnp.concatenate([row[:, None] * inv_freq, col[:, None] * inv_freq], axis=-1)
        return jnp.cos(ang), jnp.sin(ang)

    def with_ctx(tab, fill):
        return jnp.concatenate([tab, jnp.full((n_ctx, tab.shape[1]), fill, F32)], axis=0)

    cos_g, sin_g = angles(HEAD_DIM)
    cg = with_ctx(jnp.concatenate([cos_g, cos_g], axis=1), 1.0)
    sg = with_ctx(jnp.concatenate([-sin_g, sin_g], axis=1), 0.0)
    cos_m, sin_m = angles(MLA_ROPE)
    zeros = jnp.zeros((n_lat, LANES - MLA_ROPE), F32)
    cm = with_ctx(jnp.concatenate([cos_m, cos_m, zeros], axis=1), 1.0)
    sm = with_ctx(jnp.concatenate([-sin_m, sin_m, zeros], axis=1), 0.0)
    return cg, sg, cm, sm


def _pad_w_uq(w_uq):
    depth, rank, _ = w_uq.shape
    w = w_uq.reshape(depth, rank, MLA_HEADS, MLA_NOPE + MLA_ROPE)
    w = jnp.pad(w, ((0, 0), (0, 0), (0, 0), (0, MLA_QK_PAD - MLA_NOPE - MLA_ROPE)))
    return w.reshape(depth, rank, MLA_HEADS * MLA_QK_PAD)


def _dispatch(top_i, gates, tm):
    t = top_i.shape[0]
    flat_e = top_i.T.reshape(-1)
    flat_g = gates.T.reshape(-1)
    onehot = (flat_e[:, None] == jnp.arange(N_EXPERTS, dtype=jnp.int32)[None, :]).astype(jnp.int32)
    rank = jnp.sum((jnp.cumsum(onehot, axis=0) - onehot) * onehot, axis=1)
    counts = jnp.sum(onehot, axis=0)
    ptiles = (counts + tm - 1) // tm
    tile_end = jnp.cumsum(ptiles)
    tile_start = tile_end - ptiles
    pos = (tile_start[flat_e] * tm + rank).astype(jnp.int32)
    n_tiles = (2 * t) // tm + N_EXPERTS
    m_pad = n_tiles * tm
    src = (jnp.arange(m_pad, dtype=jnp.int32) % t).at[pos].set(jnp.tile(jnp.arange(t, dtype=jnp.int32), 2))
    scale = jnp.zeros((m_pad,), F32).at[pos].set(flat_g)
    tile_ids = jnp.arange(n_tiles, dtype=jnp.int32)
    last_used = jnp.max(jnp.where(ptiles > 0, jnp.arange(N_EXPERTS, dtype=jnp.int32), 0))
    tile_expert = jnp.minimum(jnp.sum((tile_end[None, :] <= tile_ids[:, None]).astype(jnp.int32), axis=1),
                              last_used).astype(jnp.int32)
    n_valid = tile_end[-1:].astype(jnp.int32)
    return src, scale.reshape(m_pad, 1), pos, tile_expert, n_valid


def kernel(x, c, ctx, c_ctx, ada_w, ada_b, ln_g, ln_b, w_in, w_out, gqa_q_gain, gqa_k_gain, lru_conv_w, lru_conv_b, lru_w_r, lru_b_r, lru_w_i, lru_b_i, lru_lambda, mla_q_gain, mla_w_uq, mla_kv_gain, mla_w_ukv, ffn_w_gate, ffn_w_up, ffn_w_down, moe_router, moe_w_gate, moe_w_up, moe_w_down):
    batch, n_lat, d = x.shape
    n_ctx = ctx.shape[1]
    depth = ada_w.shape[0]
    assert batch == 1 and ctx.shape[0] == 1
    assert n_lat % ROW_TILE == 0 and n_ctx % ROW_TILE == 0 and n_lat % n_ctx == 0 and n_lat % GRID_W == 0
    t = n_lat + n_ctx
    alpha = (2 * depth) ** 0.25

    xs = jnp.concatenate([x[0], ctx[0]], axis=0)
    cond_t = jnp.stack([c[0], c_ctx], axis=1)
    mods = _modulation(cond_t, ada_w, ada_b).reshape(depth, 2, 6, d)
    tabs = _rope_tables(n_lat, n_ctx)

    w_in_b = jnp.pad(w_in, ((0, 0), (0, 0), (0, D_IN_PAD - D_IN))).astype(BF16)
    w_out_b = w_out.astype(BF16)
    w_uq_b = _pad_w_uq(mla_w_uq).astype(BF16)
    w_ukv_b = mla_w_ukv.astype(BF16)
    w_r_b = lru_w_r.astype(BF16)
    w_i_b = lru_w_i.astype(BF16)
    router_p = jnp.pad(moe_router, ((0, 0), (0, 0), (0, LANES - N_EXPERTS)))
    ffn_tm = FFN_TM if t % FFN_TM == 0 else ROW_TILE
    moe_tm = MOE_TM if (2 * t) % MOE_TM == 0 else ROW_TILE
    n_moe, d_exp = moe_w_gate.shape[0], moe_w_gate.shape[3]
    dense_w = (ffn_w_gate.astype(BF16), ffn_w_up.astype(BF16), ffn_w_down.astype(BF16))
    moe_w = (moe_w_gate.astype(BF16).reshape(n_moe * N_EXPERTS, d, d_exp),
             moe_w_up.astype(BF16).reshape(n_moe * N_EXPERTS, d, d_exp),
             moe_w_down.astype(BF16).reshape(n_moe * N_EXPERTS, d_exp, d))
    gqa_tk = next((c for c in GQA_TK_CHOICES if t % c == 0), ATTN_TK)
    mla_tk = next((c for c in MLA_TK_CHOICES if t % c == 0), ATTN_TK)

    for l in range(depth):
        last = l == depth - 1
        mod = mods[l]
        qg, kg, vg, qm, km, vm, lx, lg = _inproj(
            xs, mod, w_in_b[l], gqa_q_gain[l][None], gqa_k_gain[l][None], mla_q_gain[l][None], w_uq_b[l],
            mla_kv_gain[l][None], w_ukv_b[l], tabs, n_lat)
        a_out = _attention(qg, kg, vg, n_lat, GQA_HEADS // GQA_KV_HEADS, "gqa", tk=gqa_tk, unroll=3, hp=2)
        m_out = _attention(qm, km, vm, n_lat, 1, "mla", tk=mla_tk, unroll=2, hp=1)
        xc = _conv(lx, lru_conv_w[l], lru_conv_b[l], n_lat)
        hf, hb = _lru_scan(xc, w_r_b[l], lru_b_r[l], w_i_b[l], lru_b_i[l], lru_lambda[l], n_lat)
        moe = l % 2 == 1
        outs = _outproj(a_out, m_out, hf, hb, lg, xs, w_out_b[l], mod, ln_g[l, 0][None], ln_b[l, 0][None],
                        alpha, n_lat, router_p[l // 2] if moe else None)
        if moe:
            x1, f_in, topi, gate = outs
            src, scale, pos, tile_expert, n_valid = _dispatch(topi[:, :2], gate[:, :2], moe_tm)
            x_sorted = _row_gather(f_in, src)
            y = _ffn(x_sorted, tile_expert + (l // 2) * N_EXPERTS, n_valid, *moe_w, scale, moe_tm)
            f_parts = _row_gather(y, pos).reshape(2, t, d)
        else:
            x1, f_in = outs
            n_tiles = t // ffn_tm
            y = _ffn(f_in, jnp.full((n_tiles,), l // 2, jnp.int32), jnp.full((1,), n_tiles, jnp.int32),
                     *dense_w, jnp.ones((t, 1), F32), ffn_tm)
            f_parts = y[None]
        xs = _ln2(x1, f_parts, mod, ln_g[l, 1][None], ln_b[l, 1][None], alpha, n_lat, n_lat if last else t)
    return xs[None]
```

```python
import functools
import math

import jax
import jax.numpy as jnp
from jax import lax
from jax.experimental import pallas as pl
from jax.experimental.pallas import tpu as pltpu

GRID_W = 64
HEAD_DIM = 128
GQA_HEADS = 8
GQA_KV_HEADS = 2
LRU_WIDTH = 512
LRU_BLOCKS = 4
LRU_BLOCK_W = LRU_WIDTH // LRU_BLOCKS
CONV_WIDTH = 4
LRU_C = 8.0
MLA_HEADS = 4
MLA_Q_RANK = 384
MLA_KV_RANK = 256
MLA_NOPE = 128
MLA_ROPE = 64
MLA_V = 128
N_EXPERTS = 8
ROPE_THETA = 10000.0
EPS = 1e-6
LOG2E = 1.4426950408889634

LANES = 128
MLA_QK_PAD = 2 * LANES
OFF_GQ = 0
OFF_GK = OFF_GQ + GQA_HEADS * HEAD_DIM
OFF_GV = OFF_GK + GQA_KV_HEADS * HEAD_DIM
OFF_LX = OFF_GV + GQA_KV_HEADS * HEAD_DIM
OFF_LG = OFF_LX + LRU_WIDTH
OFF_CQ = OFF_LG + LRU_WIDTH
OFF_CKV = OFF_CQ + MLA_Q_RANK
OFF_KR = OFF_CKV + MLA_KV_RANK
D_IN = OFF_KR + MLA_ROPE
D_IN_PAD = OFF_KR + LANES
D_GQA = GQA_HEADS * HEAD_DIM
D_MLA = MLA_HEADS * MLA_V

ROW_TILE = 256
ATTN_TQ = 512
ATTN_TK = 512
GQA_TK_CHOICES = (640,)
MLA_TK_CHOICES = (1280, 640)
FFN_TM = 640
MOE_TM = 512
FFN_TF_CHOICES = (1024, 512)
GATHER_CHUNK = 512
GATHER_UNROLL = 8
VMEM_LIMIT = 56 * 1024 * 1024

BF16 = jnp.bfloat16
F32 = jnp.float32


def _params(sem, vmem=VMEM_LIMIT):
    return pltpu.CompilerParams(dimension_semantics=sem, vmem_limit_bytes=vmem)


def _mod_kernel(cond_ref, w_ref, b_ref, o_ref):
    cc = cond_ref[...]
    s = cc * jax.nn.sigmoid(cc)
    w = w_ref[0]
    m0 = jnp.sum(w * s[:, 0:1], axis=0, keepdims=True)
    m1 = jnp.sum(w * s[:, 1:2], axis=0, keepdims=True)
    o_ref[0] = jnp.concatenate([m0, m1], axis=0) + b_ref[0]


def _modulation(cond_t, ada_w, ada_b):
    depth, d, n = ada_w.shape
    tn = 1024 if n % 1024 == 0 else n
    return pl.pallas_call(
        _mod_kernel,
        grid=(depth, n // tn),
        in_specs=[pl.BlockSpec((d, 2), lambda l, j: (0, 0)),
                  pl.BlockSpec((1, d, tn), lambda l, j: (l, 0, j)),
                  pl.BlockSpec((1, 1, tn), lambda l, j: (l, 0, j))],
        out_specs=pl.BlockSpec((1, 2, tn), lambda l, j: (l, 0, j)),
        out_shape=jax.ShapeDtypeStruct((depth, 2, n), F32),
        compiler_params=_params(("arbitrary", "arbitrary")),
        name="modulation",
    )(cond_t, ada_w, ada_b.reshape(depth, 1, n))


def _rms(x, g):
    return x * lax.rsqrt(jnp.mean(x * x, axis=-1, keepdims=True) + EPS) * g


def _inproj_kernel(x_ref, mod_ref, w_ref, qg_ref, kg_ref, mqg_ref, wuq_ref, mkg_ref, wukv_ref,
                   cg_ref, sg_ref, cm_ref, sm_ref,
                   qg_o, kg_o, vg_o, qm_o, km_o, vm_o, lx_o, lg_o):
    shift = mod_ref[0, 0:1, :]
    scale = mod_ref[0, 1:2, :]
    h = x_ref[...] * (1.0 + scale) + shift
    proj = jnp.dot(h.astype(BF16), w_ref[...], preferred_element_type=F32)

    cg = cg_ref[...]
    sg = sg_ref[...]

    def rope_gqa(v):
        return v * cg + pltpu.roll(v, HEAD_DIM // 2, 1) * sg

    q_scale = HEAD_DIM ** -0.5 * LOG2E
    for hh in range(GQA_HEADS):
        q = _rms(proj[:, OFF_GQ + hh * HEAD_DIM: OFF_GQ + (hh + 1) * HEAD_DIM], qg_ref[...])
        qg_o[hh] = (rope_gqa(q) * q_scale).astype(BF16)
    for hh in range(GQA_KV_HEADS):
        k = _rms(proj[:, OFF_GK + hh * HEAD_DIM: OFF_GK + (hh + 1) * HEAD_DIM], kg_ref[...])
        kg_o[hh] = rope_gqa(k).astype(BF16)
        vg_o[hh] = proj[:, OFF_GV + hh * HEAD_DIM: OFF_GV + (hh + 1) * HEAD_DIM].T.astype(BF16)

    lx_o[...] = proj[:, OFF_LX:OFF_LX + LRU_WIDTH]
    lg_o[...] = proj[:, OFF_LG:OFF_LG + LRU_WIDTH]

    cq = _rms(proj[:, OFF_CQ:OFF_CQ + MLA_Q_RANK], mqg_ref[...])
    qm = jnp.dot(cq.astype(BF16), wuq_ref[...], preferred_element_type=F32)
    ckv = _rms(proj[:, OFF_CKV:OFF_CKV + MLA_KV_RANK], mkg_ref[...])
    kv = jnp.dot(ckv.astype(BF16), wukv_ref[...], preferred_element_type=F32)

    cm = cm_ref[...]
    sm = sm_ref[...]

    def rope_mla(v):
        swapped = pltpu.roll(v, MLA_ROPE // 2, 1) + pltpu.roll(v, LANES - MLA_ROPE // 2, 1)
        return v * cm + swapped * sm

    kr = rope_mla(proj[:, OFF_KR:OFF_KR + LANES])
    m_scale = (MLA_NOPE + MLA_ROPE) ** -0.5 * LOG2E
    for hh in range(MLA_HEADS):
        base = hh * MLA_QK_PAD
        qn = qm[:, base:base + MLA_NOPE]
        qr = rope_mla(qm[:, base + MLA_NOPE:base + MLA_QK_PAD])
        qm_o[hh] = (jnp.concatenate([qn, qr], axis=1) * m_scale).astype(BF16)
        kvb = hh * (MLA_NOPE + MLA_V)
        km_o[hh] = jnp.concatenate([kv[:, kvb:kvb + MLA_NOPE], kr], axis=1).astype(BF16)
        vm_o[hh] = kv[:, kvb + MLA_NOPE:kvb + MLA_NOPE + MLA_V].T.astype(BF16)


def _inproj(x, mod, w_in, q_gain, k_gain, mla_q_gain, w_uq, mla_kv_gain, w_ukv, tabs, n_lat):
    t, d = x.shape
    tm = ROW_TILE
    n_lat_tiles = n_lat // tm
    full = lambda a: pl.BlockSpec(a.shape, lambda i: (0,) * a.ndim)
    row = lambda w: pl.BlockSpec((tm, w), lambda i: (i, 0))
    heads = lambda n, w: pl.BlockSpec((n, tm, w), lambda i: (0, i, 0))
    heads_t = lambda n, w: pl.BlockSpec((n, w, tm), lambda i: (0, 0, i))
    cg, sg, cm, sm = tabs
    return pl.pallas_call(
        _inproj_kernel,
        grid=(t // tm,),
        in_specs=[row(d),
                  pl.BlockSpec((1, 6, d), lambda i: (jnp.where(i < n_lat_tiles, 0, 1), 0, 0)),
                  full(w_in), full(q_gain), full(k_gain), full(mla_q_gain), full(w_uq), full(mla_kv_gain),
                  full(w_ukv), row(LANES), row(LANES), row(LANES), row(LANES)],
        out_specs=[heads(GQA_HEADS, HEAD_DIM), heads(GQA_KV_HEADS, HEAD_DIM), heads_t(GQA_KV_HEADS, HEAD_DIM),
                   heads(MLA_HEADS, MLA_QK_PAD), heads(MLA_HEADS, MLA_QK_PAD), heads_t(MLA_HEADS, MLA_V),
                   row(LRU_WIDTH), row(LRU_WIDTH)],
        out_shape=[jax.ShapeDtypeStruct((GQA_HEADS, t, HEAD_DIM), BF16),
                   jax.ShapeDtypeStruct((GQA_KV_HEADS, t, HEAD_DIM), BF16),
                   jax.ShapeDtypeStruct((GQA_KV_HEADS, HEAD_DIM, t), BF16),
                   jax.ShapeDtypeStruct((MLA_HEADS, t, MLA_QK_PAD), BF16),
                   jax.ShapeDtypeStruct((MLA_HEADS, t, MLA_QK_PAD), BF16),
                   jax.ShapeDtypeStruct((MLA_HEADS, MLA_V, t), BF16),
                   jax.ShapeDtypeStruct((t, LRU_WIDTH), F32),
                   jax.ShapeDtypeStruct((t, LRU_WIDTH), F32)],
        compiler_params=_params(("arbitrary",)),
        name="inproj",
    )(x, mod, w_in, q_gain, k_gain, mla_q_gain, w_uq, mla_kv_gain, w_ukv, cg, sg, cm, sm)


ONES_ROWS = 16


def _scores_t(q, k_ref, start, size):
    k = k_ref[0, pl.ds(start, size), :]
    return lax.dot_general(k, q, (((1,), (1,)), ((), ())), preferred_element_type=F32)


def _softmax_step(st, vt_ref, start, size, carry):
    m, acc = carry
    ones = jnp.where(lax.broadcasted_iota(jnp.int32, (ONES_ROWS, size), 0) == 0, 1.0, 0.0).astype(BF16)
    vt = jnp.concatenate([vt_ref[0, :, pl.ds(start, size)], ones], axis=0)
    m_new = jnp.maximum(m, jnp.max(st, axis=0, keepdims=True))
    alpha = jnp.exp2(m - m_new)
    pt = jnp.exp2(st - m_new).astype(BF16)
    acc = alpha * acc + jnp.dot(vt, pt, preferred_element_type=F32)
    return m_new, acc


def _attn_init(tq, dv):
    return jnp.full((1, tq), -1e30, F32), jnp.zeros((dv + ONES_ROWS, tq), F32)


def _attn_finish(carry, o_ref):
    _, acc = carry
    dv = acc.shape[0] - ONES_ROWS
    o_ref[...] = (acc[:dv, :] / acc[dv:dv + 1, :]).T.astype(o_ref.dtype)


def _attn_latent_kernel(n_main, tk, tail, unroll, q_ref, k_ref, vt_ref, o_ref, s_ref):
    hp, tq = q_ref.shape[0], q_ref.shape[1]
    dv = vt_ref.shape[1]
    qs = [q_ref[h] for h in range(hp)]
    step = _softmax_step

    def put(h, slot, st):
        s_ref[h, slot] = st

    def get(h, slot):
        return s_ref[h, slot]

    def body(jj, carries):
        c0 = pl.multiple_of(jj * (2 * tk), 2 * tk)
        carries = list(carries)
        for h in range(hp):
            put(h, 1, _scores_t(qs[h], k_ref, c0 + tk, tk))
            carries[h] = step(get(h, 0), vt_ref, c0, tk, carries[h])
        for h in range(hp):
            put(h, 0, _scores_t(qs[h], k_ref, c0 + 2 * tk, tk))
            carries[h] = step(get(h, 1), vt_ref, c0 + tk, tk, carries[h])
        return tuple(carries)

    carries = tuple(_attn_init(tq, dv) for _ in range(hp))
    for h in range(hp):
        put(h, 0, _scores_t(qs[h], k_ref, 0, tk))
    n_loop = (n_main - 1) // 2
    carries = list(lax.fori_loop(0, n_loop, body, carries, unroll=unroll))
    c0 = 2 * n_loop * tk
    two_left = n_main - 2 * n_loop == 2
    for h in range(hp):
        if two_left:
            put(h, 1, _scores_t(qs[h], k_ref, c0 + tk, tk))
        s_tail = _scores_t(qs[h], k_ref, n_main * tk, tail) if tail else None
        carries[h] = step(get(h, 0), vt_ref, c0, tk, carries[h])
        if two_left:
            carries[h] = step(get(h, 1), vt_ref, c0 + tk, tk, carries[h])
        if tail:
            carries[h] = step(s_tail, vt_ref, n_main * tk, tail, carries[h])
        _attn_finish(carries[h], o_ref.at[:, h * dv:(h + 1) * dv])


def _attn_context_kernel(q_ref, k_ref, vt_ref, o_ref):
    n = k_ref.shape[1]
    carry = _attn_init(q_ref.shape[1], vt_ref.shape[1])
    carry = _softmax_step(_scores_t(q_ref[0], k_ref, 0, n), vt_ref, 0, n, carry)
    _attn_finish(carry, o_ref)


def _attention(q, k, vt, n_lat, group, name, tk, unroll, hp):
    n_heads, t, dk = q.shape
    dv = vt.shape[1]
    n_ctx = t - n_lat
    tq = ATTN_TQ if n_lat % ATTN_TQ == 0 else ROW_TILE
    if t % tk == 0:
        n_main, tail = t // tk, 0
    else:
        tk = tk if n_lat % tk == 0 else ROW_TILE
        n_main, tail = n_lat // tk, n_ctx
    assert n_main * tk + tail == t and group % hp == 0 and n_heads % hp == 0
    lat = pl.pallas_call(
        functools.partial(_attn_latent_kernel, n_main, tk, tail, unroll),
        grid=(n_heads // hp, n_lat // tq),
        in_specs=[pl.BlockSpec((hp, tq, dk), lambda h, i: (h, i, 0)),
                  pl.BlockSpec((1, t, dk), lambda h, i: ((h * hp) // group, 0, 0)),
                  pl.BlockSpec((1, dv, t), lambda h, i: ((h * hp) // group, 0, 0))],
        out_specs=pl.BlockSpec((tq, hp * dv), lambda h, i: (i, h)),
        out_shape=jax.ShapeDtypeStruct((n_lat, n_heads * dv), BF16),
        scratch_shapes=[pltpu.VMEM((hp, 2, tk, tq), F32)],
        compiler_params=_params(("arbitrary", "arbitrary")),
        name=name + "_latent",
    )(q, k, vt)
    cblk = n_lat // n_ctx
    ctx = pl.pallas_call(
        _attn_context_kernel,
        grid=(n_heads,),
        in_specs=[pl.BlockSpec((1, n_ctx, dk), lambda h: (h, cblk, 0)),
                  pl.BlockSpec((1, n_ctx, dk), lambda h: (h // group, cblk, 0)),
                  pl.BlockSpec((1, dv, n_ctx), lambda h: (h // group, 0, cblk))],
        out_specs=pl.BlockSpec((n_ctx, dv), lambda h: (0, h)),
        out_shape=jax.ShapeDtypeStruct((n_ctx, n_heads * dv), BF16),
        compiler_params=_params(("arbitrary",)),
        name=name + "_context",
    )(q, k, vt)
    return lat, ctx


def _conv_kernel(n_lat, t_total, xp_ref, xc_ref, xn_ref, w_ref, b_ref, o_ref):
    i = pl.program_id(0)
    r = xc_ref.shape[0]
    halo = 8
    ext = jnp.concatenate([xp_ref[r - halo:r, :], xc_ref[...], xn_ref[0:halo, :]], axis=0)
    rows = i * r + lax.broadcasted_iota(jnp.int32, (r, 1), 0)
    in_ctx = rows >= n_lat
    seg_lo = jnp.where(in_ctx, n_lat, 0)
    seg_hi = jnp.where(in_ctx, t_total, n_lat)
    left = CONV_WIDTH // 2
    out = jnp.broadcast_to(b_ref[...], (r, xc_ref.shape[1]))
    for tap in range(CONV_WIDTH):
        off = tap - left
        shifted = ext if off == 0 else pltpu.roll(ext, (-off) % (r + 2 * halo), 0)
        xs = shifted[halo:halo + r, :]
        src = rows + off
        valid = jnp.logical_and(src >= seg_lo, src < seg_hi)
        out = out + jnp.where(valid, xs, 0.0) * w_ref[tap:tap + 1, :]
    o_ref[...] = out


def _conv(lx, conv_w, conv_b, n_lat):
    t, w = lx.shape
    r = ROW_TILE
    nt = t // r
    return pl.pallas_call(
        functools.partial(_conv_kernel, n_lat, t),
        grid=(nt,),
        in_specs=[pl.BlockSpec((r, w), lambda i: (jnp.maximum(i - 1, 0), 0)),
                  pl.BlockSpec((r, w), lambda i: (i, 0)),
                  pl.BlockSpec((r, w), lambda i: (jnp.minimum(i + 1, nt - 1), 0)),
                  pl.BlockSpec(conv_w.shape, lambda i: (0, 0)),
                  pl.BlockSpec((1, w), lambda i: (0, 0))],
        out_specs=pl.BlockSpec((r, w), lambda i: (i, 0)),
        out_shape=jax.ShapeDtypeStruct((t, w), F32),
        compiler_params=_params(("arbitrary",)),
        name="lru_conv",
    )(lx, lx, lx, conv_w, conv_b.reshape(1, w))


def _lru_coeffs(x, d, wr_ref, br_ref, wi_ref, bi_ref, lam_ref):
    xb = x.astype(BF16)
    r_parts, i_parts = [], []
    for n in range(LRU_BLOCKS):
        xs = xb[:, n * LRU_BLOCK_W:(n + 1) * LRU_BLOCK_W]
        r_parts.append(jnp.dot(xs, wr_ref[d, n], preferred_element_type=F32))
        i_parts.append(jnp.dot(xs, wi_ref[d, n], preferred_element_type=F32))
    r = jax.nn.sigmoid(jnp.concatenate(r_parts, axis=1) + br_ref[d:d + 1, :])
    ig = jax.nn.sigmoid(jnp.concatenate(i_parts, axis=1) + bi_ref[d:d + 1, :])
    neg_lam = -lam_ref[d:d + 1, :]
    softplus = jnp.maximum(neg_lam, 0.0) + jnp.log1p(jnp.exp(-jnp.abs(neg_lam)))
    log_a = -LRU_C * r * softplus
    a = jnp.exp(log_a)
    u = jnp.sqrt(-jnp.tanh(log_a) * (a * a + 1.0)) * ig * x
    return a, u


def _tile_scan(a, u, reverse):
    r = a.shape[0]
    rows = lax.broadcasted_iota(jnp.int32, (r, 1), 0)
    d = 1
    while d < r:
        if reverse:
            a_sh = pltpu.roll(a, r - d, 0)
            u_sh = pltpu.roll(u, r - d, 0)
            ok = rows < r - d
        else:
            a_sh = pltpu.roll(a, d, 0)
            u_sh = pltpu.roll(u, d, 0)
            ok = rows >= d
        u = jnp.where(ok, a * u_sh + u, u)
        a = jnp.where(ok, a * a_sh, a)
        d *= 2
    return a, u


def _scan_kernel(nc_tiles, xf_ref, xb_ref, wr_ref, br_ref, wi_ref, bi_ref, lam_ref, hf_ref, hb_ref, carry_ref):
    s = pl.program_id(0)
    r = xf_ref.shape[0]

    @pl.when(s == 0)
    def _():
        carry_ref[...] = jnp.zeros_like(carry_ref)

    del nc_tiles
    a, u = _lru_coeffs(xf_ref[...], 0, wr_ref, br_ref, wi_ref, bi_ref, lam_ref)
    a, u = _tile_scan(a, u, False)
    h = a * carry_ref[0:1, :] + u
    hf_ref[...] = h
    carry_ref[0:1, :] = h[r - 1:r, :]

    a, u = _lru_coeffs(xb_ref[...], 1, wr_ref, br_ref, wi_ref, bi_ref, lam_ref)
    a, u = _tile_scan(a, u, True)
    h = a * carry_ref[1:2, :] + u
    hb_ref[...] = h
    carry_ref[1:2, :] = h[0:1, :]


def _lru_scan(xc, w_r, b_r, w_i, b_i, lam, n_lat):
    t, w = xc.shape
    r = ROW_TILE
    nt, nl = t // r, n_lat // r
    nc = nt - nl
    fwd = lambda s: (jnp.where(s < nc, nl + s, s - nc), 0)
    bwd = lambda s: (jnp.where(s < nc, nt - 1 - s, nl - 1 - (s - nc)), 0)
    full = lambda a: pl.BlockSpec(a.shape, lambda s: (0,) * a.ndim)
    return pl.pallas_call(
        functools.partial(_scan_kernel, nc),
        grid=(nt,),
        in_specs=[pl.BlockSpec((r, w), fwd), pl.BlockSpec((r, w), bwd),
                  full(w_r), full(b_r), full(w_i), full(b_i), full(lam)],
        out_specs=[pl.BlockSpec((r, w), fwd), pl.BlockSpec((r, w), bwd)],
        out_shape=[jax.ShapeDtypeStruct((t, w), F32), jax.ShapeDtypeStruct((t, w), F32)],
        scratch_shapes=[pltpu.VMEM((8, w), F32)],
        compiler_params=_params(("arbitrary",)),
        name="lru_scan",
    )(xc, xc, w_r, b_r, w_i, b_i, lam)


def _layer_norm(y, g, b):
    mu = jnp.mean(y, axis=-1, keepdims=True)
    yc = y - mu
    var = jnp.mean(yc * yc, axis=-1, keepdims=True)
    return yc * lax.rsqrt(var + EPS) * g + b


def _gelu_tanh(x):
    return 0.5 * x * (1.0 + jnp.tanh(math.sqrt(2.0 / math.pi) * (x + 0.044715 * (x * x * x))))


def _outproj_kernel(alpha, with_router, n_lat_tiles, al_ref, ac_ref, ml_ref, mc_ref, hf_ref, hb_ref, lg_ref, x_ref,
                    w_ref, mod_ref, g_ref, b_ref, *rest):
    if with_router:
        router_ref, x1_o, fin_o, topi_o, gate_o = rest
    else:
        x1_o, fin_o = rest
    is_lat = pl.program_id(0) < n_lat_tiles
    a = jnp.where(is_lat, al_ref[...], ac_ref[...])
    m = jnp.where(is_lat, ml_ref[...], mc_ref[...])
    rec = _gelu_tanh(lg_ref[...]) * (hf_ref[...] + hb_ref[...])
    mix = jnp.dot(a, w_ref[0:D_GQA, :], preferred_element_type=F32)
    mix += jnp.dot(rec.astype(BF16), w_ref[D_GQA:D_GQA + LRU_WIDTH, :], preferred_element_type=F32)
    mix += jnp.dot(m, w_ref[D_GQA + LRU_WIDTH:, :], preferred_element_type=F32)
    y = alpha * x_ref[...] + mod_ref[0, 2:3, :] * mix
    x1 = _layer_norm(y, g_ref[...], b_ref[...])
    x1_o[...] = x1
    f_in = x1 * (1.0 + mod_ref[0, 4:5, :]) + mod_ref[0, 3:4, :]
    fin_o[...] = f_in
    if with_router:
        f_hi = f_in.astype(BF16)
        f_lo = (f_in - f_hi.astype(F32)).astype(BF16)
        r = router_ref[...]
        r_hi = r.astype(BF16)
        r_lo = (r - r_hi.astype(F32)).astype(BF16)
        logits = (jnp.dot(f_hi, r_hi, preferred_element_type=F32) + jnp.dot(f_hi, r_lo, preferred_element_type=F32)
                  + jnp.dot(f_lo, r_hi, preferred_element_type=F32))
        lane = lax.broadcasted_iota(jnp.int32, logits.shape, 1).astype(F32)
        neg = jnp.float32(-jnp.inf)
        lgt = jnp.where(lane < N_EXPERTS, logits, neg)
        v1 = jnp.max(lgt, axis=-1, keepdims=True)
        i1 = jnp.min(jnp.where(lgt == v1, lane, float(LANES)), axis=-1, keepdims=True)
        lgt2 = jnp.where(lane == i1, neg, lgt)
        v2 = jnp.max(lgt2, axis=-1, keepdims=True)
        i2 = jnp.min(jnp.where(lgt2 == v2, lane, float(LANES)), axis=-1, keepdims=True)
        e2 = jnp.exp(v2 - v1)
        den = 1.0 + e2
        topi_o[...] = jnp.where(lane == 0.0, i1, jnp.where(lane == 1.0, i2, 0.0)).astype(jnp.int32)
        gate_o[...] = jnp.where(lane == 0.0, 1.0 / den, jnp.where(lane == 1.0, e2 / den, 0.0))


def _outproj(a, m, hf, hb, lg, x, w_out, mod, ln_g, ln_b, alpha, n_lat, router):
    t, d = x.shape
    tm = ROW_TILE
    n_lat_tiles = n_lat // tm
    row = lambda w: pl.BlockSpec((tm, w), lambda i: (i, 0))
    lat_row = lambda w: pl.BlockSpec((tm, w), lambda i: (jnp.minimum(i, n_lat_tiles - 1), 0))
    ctx_row = lambda w: pl.BlockSpec((tm, w), lambda i: (jnp.maximum(i - n_lat_tiles, 0), 0))
    full = lambda arr: pl.BlockSpec(arr.shape, lambda i: (0,) * arr.ndim)
    in_specs = [lat_row(D_GQA), ctx_row(D_GQA), lat_row(D_MLA), ctx_row(D_MLA),
                row(LRU_WIDTH), row(LRU_WIDTH), row(LRU_WIDTH), row(d),
                full(w_out), pl.BlockSpec((1, 6, d), lambda i: (jnp.where(i < n_lat_tiles, 0, 1), 0, 0)),
                full(ln_g), full(ln_b)]
    out_specs = [row(d), row(d)]
    out_shape = [jax.ShapeDtypeStruct((t, d), F32), jax.ShapeDtypeStruct((t, d), F32)]
    args = [a[0], a[1], m[0], m[1], hf, hb, lg, x, w_out, mod, ln_g, ln_b]
    if router is not None:
        in_specs.append(full(router))
        args.append(router)
        out_specs += [row(LANES), row(LANES)]
        out_shape += [jax.ShapeDtypeStruct((t, LANES), jnp.int32), jax.ShapeDtypeStruct((t, LANES), F32)]
    return pl.pallas_call(
        functools.partial(_outproj_kernel, alpha, router is not None, n_lat_tiles),
        grid=(t // tm,),
        in_specs=in_specs, out_specs=out_specs, out_shape=out_shape,
        compiler_params=_params(("arbitrary",)),
        name="outproj",
    )(*args)


def _ffn_kernel(te_ref, nv_ref, x_ref, wg_ref, wu_ref, wd_ref, sc_ref, o_ref, acc_ref):
    del te_ref
    i, j = pl.program_id(0), pl.program_id(1)

    @pl.when(j == 0)
    def _():
        acc_ref[...] = jnp.zeros_like(acc_ref)

    @pl.when(i < nv_ref[0])
    def _():
        xb = x_ref[...].astype(BF16)
        g = jnp.dot(xb, wg_ref[0], preferred_element_type=F32)
        u = jnp.dot(xb, wu_ref[0], preferred_element_type=F32)
        hidden = (g * jax.nn.sigmoid(g)) * u
        acc_ref[...] += jnp.dot(hidden.astype(BF16), wd_ref[0], preferred_element_type=F32)

    @pl.when(j == pl.num_programs(1) - 1)
    def _():
        o_ref[...] = acc_ref[...] * sc_ref[...]


def _ffn(x, tile_expert, n_valid, w_gate, w_up, w_down, row_scale, tm):
    m_rows, d = x.shape
    f = w_gate.shape[2]
    tf = next((c for c in FFN_TF_CHOICES if f % c == 0), f)
    nj = f // tf
    col = lambda i, j, nv: jnp.where(i < nv[0], j, nj - 1)
    grid_spec = pltpu.PrefetchScalarGridSpec(
        num_scalar_prefetch=2,
        grid=(m_rows // tm, nj),
        in_specs=[pl.BlockSpec((tm, d), lambda i, j, te, nv: (i, 0)),
                  pl.BlockSpec((1, d, tf), lambda i, j, te, nv: (te[i], 0, col(i, j, nv))),
                  pl.BlockSpec((1, d, tf), lambda i, j, te, nv: (te[i], 0, col(i, j, nv))),
                  pl.BlockSpec((1, tf, d), lambda i, j, te, nv: (te[i], col(i, j, nv), 0)),
                  pl.BlockSpec((tm, 1), lambda i, j, te, nv: (i, 0))],
        out_specs=pl.BlockSpec((tm, d), lambda i, j, te, nv: (i, 0)),
        scratch_shapes=[pltpu.VMEM((tm, d), F32)],
    )
    return pl.pallas_call(
        _ffn_kernel,
        grid_spec=grid_spec,
        out_shape=jax.ShapeDtypeStruct((m_rows, d), F32),
        compiler_params=_params(("arbitrary", "arbitrary")),
        name="ffn",
    )(tile_expert, n_valid, x, w_gate, w_up, w_down, row_scale)


def _gather_kernel(idx_ref, src_ref, dst_ref, sem):
    base = pl.program_id(0) * GATHER_CHUNK

    def issue(g, carry):
        for k in range(GATHER_UNROLL):
            r = g * GATHER_UNROLL + k
            pltpu.make_async_copy(src_ref.at[pl.ds(idx_ref[base + r], 1)], dst_ref.at[pl.ds(r, 1)], sem).start(
                priority=k % 2)
        return carry

    lax.fori_loop(0, GATHER_CHUNK // GATHER_UNROLL, issue, 0)
    pltpu.make_async_copy(src_ref.at[pl.ds(0, GATHER_CHUNK)], dst_ref, sem).wait()


def _row_gather(src, idx):
    m_rows = idx.shape[0]
    assert m_rows % GATHER_CHUNK == 0
    grid_spec = pltpu.PrefetchScalarGridSpec(
        num_scalar_prefetch=1,
        grid=(m_rows // GATHER_CHUNK,),
        in_specs=[pl.BlockSpec(memory_space=pl.ANY)],
        out_specs=pl.BlockSpec((GATHER_CHUNK, src.shape[1]), lambda c, idx: (c, 0)),
        scratch_shapes=[pltpu.SemaphoreType.DMA(())],
    )
    return pl.pallas_call(
        _gather_kernel,
        grid_spec=grid_spec,
        out_shape=jax.ShapeDtypeStruct((m_rows, src.shape[1]), src.dtype),
        compiler_params=_params(("arbitrary",)),
        name="row_gather",
    )(idx, src)


def _ln2_kernel(alpha, n_parts, x_ref, f_ref, mod_ref, g_ref, b_ref, o_ref):
    f = f_ref[0]
    for p in range(1, n_parts):
        f = f + f_ref[p]
    y = alpha * x_ref[...] + mod_ref[0, 5:6, :] * f
    o_ref[...] = _layer_norm(y, g_ref[...], b_ref[...])


def _ln2(x1, f_parts, mod, ln_g, ln_b, alpha, n_lat, n_rows):
    n_parts, _, d = f_parts.shape
    tm = ROW_TILE
    n_lat_tiles = n_lat // tm
    full = lambda arr: pl.BlockSpec(arr.shape, lambda i: (0,) * arr.ndim)
    return pl.pallas_call(
        functools.partial(_ln2_kernel, alpha, n_parts),
        grid=(n_rows // tm,),
        in_specs=[pl.BlockSpec((tm, d), lambda i: (i, 0)),
                  pl.BlockSpec((n_parts, tm, d), lambda i: (0, i, 0)),
                  pl.BlockSpec((1, 6, d), lambda i: (jnp.where(i < n_lat_tiles, 0, 1), 0, 0)),
                  full(ln_g), full(ln_b)],
        out_specs=pl.BlockSpec((tm, d), lambda i: (i, 0)),
        out_shape=jax.ShapeDtypeStruct((n_rows, d), F32),
        compiler_params=_params(("arbitrary",)),
        name="ln2",
    )(x1, f_parts, mod, ln_g, ln_b)


def _rope_tables(n_lat, n_ctx):
    rows = n_lat // GRID_W

    def angles(rot_dim):
        quarter = rot_dim // 4
        inv_freq = ROPE_THETA ** (-jnp.arange(quarter, dtype=F32) / quarter)
        row = jnp.repeat(jnp.arange(rows, dtype=F32), GRID_W)
        col = jnp.tile(jnp.arange(GRID_W, dtype=F32), rows)
        ang = jnp.concatenate([row[:, None] * inv_freq, col[:, None] * inv_freq], axis=-1)
        return jnp.cos(ang), jnp.sin(ang)

    def with_ctx(tab, fill):
        return jnp.concatenate([tab, jnp.full((n_ctx, tab.shape[1]), fill, F32)], axis=0)

    cos_g, sin_g = angles(HEAD_DIM)
    cg = with_ctx(jnp.concatenate([cos_g, cos_g], axis=1), 1.0)
    sg = with_ctx(jnp.concatenate([-sin_g, sin_g], axis=1), 0.0)
    cos_m, sin_m = angles(MLA_ROPE)
    zeros = jnp.zeros((n_lat, LANES - MLA_ROPE), F32)
    cm = with_ctx(jnp.concatenate([cos_m, cos_m, zeros], axis=1), 1.0)
    sm = with_ctx(jnp.concatenate([-sin_m, sin_m, zeros], axis=1), 0.0)
    return cg, sg, cm, sm


def _pad_w_uq(w_uq):
    depth, rank, _ = w_uq.shape
    w = w_uq.reshape(depth, rank, MLA_HEADS, MLA_NOPE + MLA_ROPE)
    w = jnp.pad(w, ((0, 0), (0, 0), (0, 0), (0, MLA_QK_PAD - MLA_NOPE - MLA_ROPE)))
    return w.reshape(depth, rank, MLA_HEADS * MLA_QK_PAD)


def _dispatch(top_i, gates, tm):
    t = top_i.shape[0]
    flat_e = top_i.T.reshape(-1)
    flat_g = gates.T.reshape(-1)
    onehot = (flat_e[:, None] == jnp.arange(N_EXPERTS, dtype=jnp.int32)[None, :]).astype(jnp.int32)
    rank = jnp.sum((jnp.cumsum(onehot, axis=0) - onehot) * onehot, axis=1)
    counts = jnp.sum(onehot, axis=0)
    ptiles = (counts + tm - 1) // tm
    tile_end = jnp.cumsum(ptiles)
    tile_start = tile_end - ptiles
    pos = (tile_start[flat_e] * tm + rank).astype(jnp.int32)
    n_tiles = (2 * t) // tm + N_EXPERTS
    m_pad = n_tiles * tm
    src = (jnp.arange(m_pad, dtype=jnp.int32) % t).at[pos].set(jnp.tile(jnp.arange(t, dtype=jnp.int32), 2))
    scale = jnp.zeros((m_pad,), F32).at[pos].set(flat_g)
    tile_ids = jnp.arange(n_tiles, dtype=jnp.int32)
    last_used = jnp.max(jnp.where(ptiles > 0, jnp.arange(N_EXPERTS, dtype=jnp.int32), 0))
    tile_expert = jnp.minimum(jnp.sum((tile_end[None, :] <= tile_ids[:, None]).astype(jnp.int32), axis=1),
                              last_used).astype(jnp.int32)
    n_valid = tile_end[-1:].astype(jnp.int32)
    return src, scale.reshape(m_pad, 1), pos, tile_expert, n_valid


def kernel(x, c, ctx, c_ctx, ada_w, ada_b, ln_g, ln_b, w_in, w_out, gqa_q_gain, gqa_k_gain, lru_conv_w, lru_conv_b, lru_w_r, lru_b_r, lru_w_i, lru_b_i, lru_lambda, mla_q_gain, mla_w_uq, mla_kv_gain, mla_w_ukv, ffn_w_gate, ffn_w_up, ffn_w_down, moe_router, moe_w_gate, moe_w_up, moe_w_down):
    batch, n_lat, d = x.shape
    n_ctx = ctx.shape[1]
    depth = ada_w.shape[0]
    assert batch == 1 and ctx.shape[0] == 1
    assert n_lat % ROW_TILE == 0 and n_ctx % ROW_TILE == 0 and n_lat % n_ctx == 0 and n_lat % GRID_W == 0
    t = n_lat + n_ctx
    alpha = (2 * depth) ** 0.25

    xs = jnp.concatenate([x[0], ctx[0]], axis=0)
    cond_t = jnp.stack([c[0], c_ctx], axis=1)
    mods = _modulation(cond_t, ada_w, ada_b).reshape(depth, 2, 6, d)
    tabs = _rope_tables(n_lat, n_ctx)

    w_in_b = jnp.pad(w_in, ((0, 0), (0, 0), (0, D_IN_PAD - D_IN))).astype(BF16)
    w_out_b = w_out.astype(BF16)
    w_uq_b = _pad_w_uq(mla_w_uq).astype(BF16)
    w_ukv_b = mla_w_ukv.astype(BF16)
    w_r_b = lru_w_r.astype(BF16)
    w_i_b = lru_w_i.astype(BF16)
    router_p = jnp.pad(moe_router, ((0, 0), (0, 0), (0, LANES - N_EXPERTS)))
    ffn_tm = FFN_TM if t % FFN_TM == 0 else ROW_TILE
    moe_tm = MOE_TM if (2 * t) % MOE_TM == 0 else ROW_TILE
    n_moe, d_exp = moe_w_gate.shape[0], moe_w_gate.shape[3]
    dense_w = (ffn_w_gate.astype(BF16), ffn_w_up.astype(BF16), ffn_w_down.astype(BF16))
    moe_w = (moe_w_gate.astype(BF16).reshape(n_moe * N_EXPERTS, d, d_exp),
             moe_w_up.astype(BF16).reshape(n_moe * N_EXPERTS, d, d_exp),
             moe_w_down.astype(BF16).reshape(n_moe * N_EXPERTS, d_exp, d))
    gqa_tk = next((c for c in GQA_TK_CHOICES if t % c == 0), ATTN_TK)
    mla_tk = next((c for c in MLA_TK_CHOICES if t % c == 0), ATTN_TK)

    for l in range(depth):
        last = l == depth - 1
        mod = mods[l]
        qg, kg, vg, qm, km, vm, lx, lg = _inproj(
            xs, mod, w_in_b[l], gqa_q_gain[l][None], gqa_k_gain[l][None], mla_q_gain[l][None], w_uq_b[l],
            mla_kv_gain[l][None], w_ukv_b[l], tabs, n_lat)
        a_out = _attention(qg, kg, vg, n_lat, GQA_HEADS // GQA_KV_HEADS, "gqa", tk=gqa_tk, unroll=6, hp=2)
        m_out = _attention(qm, km, vm, n_lat, 1, "mla", tk=mla_tk, unroll=2, hp=1)
        xc = _conv(lx, lru_conv_w[l], lru_conv_b[l], n_lat)
        hf, hb = _lru_scan(xc, w_r_b[l], lru_b_r[l], w_i_b[l], lru_b_i[l], lru_lambda[l], n_lat)
        moe = l % 2 == 1
        outs = _outproj(a_out, m_out, hf, hb, lg, xs, w_out_b[l], mod, ln_g[l, 0][None], ln_b[l, 0][None],
                        alpha, n_lat, router_p[l // 2] if moe else None)
        if moe:
            x1, f_in, topi, gate = outs
            src, scale, pos, tile_expert, n_valid = _dispatch(topi[:, :2], gate[:, :2], moe_tm)
            x_sorted = _row_gather(f_in, src)
            y = _ffn(x_sorted, tile_expert + (l // 2) * N_EXPERTS, n_valid, *moe_w, scale, moe_tm)
            f_parts = _row_gather(y, pos).reshape(2, t, d)
        else:
            x1, f_in = outs
            n_tiles = t // ffn_tm
            y = _ffn(f_in, jnp.full((n_tiles,), l // 2, jnp.int32), jnp.full((1,), n_tiles, jnp.int32),
                     *dense_w, jnp.ones((t, 1), F32), ffn_tm)
            f_parts = y[None]
        xs = _ln2(x1, f_parts, mod, ln_g[l, 1][None], ln_b[l, 1][None], alpha, n_lat, n_lat if last else t)
    return xs[None]
```

```python
import functools
import math

import jax
import jax.numpy as jnp
from jax import lax
from jax.experimental import pallas as pl
from jax.experimental.pallas import tpu as pltpu

GRID_W = 64
HEAD_DIM = 128
GQA_HEADS = 8
GQA_KV_HEADS = 2
LRU_WIDTH = 512
LRU_BLOCKS = 4
LRU_BLOCK_W = LRU_WIDTH // LRU_BLOCKS
CONV_WIDTH = 4
LRU_C = 8.0
MLA_HEADS = 4
MLA_Q_RANK = 384
MLA_KV_RANK = 256
MLA_NOPE = 128
MLA_ROPE = 64
MLA_V = 128
N_EXPERTS = 8
ROPE_THETA = 10000.0
EPS = 1e-6
LOG2E = 1.4426950408889634

LANES = 128
MLA_QK_PAD = 2 * LANES
OFF_GQ = 0
OFF_GK = OFF_GQ + GQA_HEADS * HEAD_DIM
OFF_GV = OFF_GK + GQA_KV_HEADS * HEAD_DIM
OFF_LX = OFF_GV + GQA_KV_HEADS * HEAD_DIM
OFF_LG = OFF_LX + LRU_WIDTH
OFF_CQ = OFF_LG + LRU_WIDTH
OFF_CKV = OFF_CQ + MLA_Q_RANK
OFF_KR = OFF_CKV + MLA_KV_RANK
D_IN = OFF_KR + MLA_ROPE
D_IN_PAD = OFF_KR + LANES
D_GQA = GQA_HEADS * HEAD_DIM
D_MLA = MLA_HEADS * MLA_V

ROW_TILE = 256
ATTN_TQ = 512
ATTN_TK = 512
GQA_TK_CHOICES = (640,)
MLA_TK_CHOICES = (1280, 640)
FFN_TM = 640
MOE_TM = 512
FFN_TF_CHOICES = (1024, 512)
GATHER_CHUNK = 512
GATHER_UNROLL = 8
VMEM_LIMIT = 56 * 1024 * 1024

BF16 = jnp.bfloat16
F32 = jnp.float32


def _params(sem, vmem=VMEM_LIMIT):
    return pltpu.CompilerParams(dimension_semantics=sem, vmem_limit_bytes=vmem)


def _mod_kernel(cond_ref, w_ref, b_ref, o_ref):
    cc = cond_ref[...]
    s = cc * jax.nn.sigmoid(cc)
    w = w_ref[0]
    m0 = jnp.sum(w * s[:, 0:1], axis=0, keepdims=True)
    m1 = jnp.sum(w * s[:, 1:2], axis=0, keepdims=True)
    o_ref[0] = jnp.concatenate([m0, m1], axis=0) + b_ref[0]


def _modulation(cond_t, ada_w, ada_b):
    depth, d, n = ada_w.shape
    tn = 1024 if n % 1024 == 0 else n
    return pl.pallas_call(
        _mod_kernel,
        grid=(depth, n // tn),
        in_specs=[pl.BlockSpec((d, 2), lambda l, j: (0, 0)),
                  pl.BlockSpec((1, d, tn), lambda l, j: (l, 0, j)),
                  pl.BlockSpec((1, 1, tn), lambda l, j: (l, 0, j))],
        out_specs=pl.BlockSpec((1, 2, tn), lambda l, j: (l, 0, j)),
        out_shape=jax.ShapeDtypeStruct((depth, 2, n), F32),
        compiler_params=_params(("arbitrary", "arbitrary")),
        name="modulation",
    )(cond_t, ada_w, ada_b.reshape(depth, 1, n))


def _rms(x, g):
    return x * lax.rsqrt(jnp.mean(x * x, axis=-1, keepdims=True) + EPS) * g


def _inproj_kernel(x_ref, mod_ref, w_ref, qg_ref, kg_ref, mqg_ref, wuq_ref, mkg_ref, wukv_ref,
                   cg_ref, sg_ref, cm_ref, sm_ref,
                   qg_o, kg_o, vg_o, qm_o, km_o, vm_o, lx_o, lg_o):
    shift = mod_ref[0, 0:1, :]
    scale = mod_ref[0, 1:2, :]
    h = x_ref[...] * (1.0 + scale) + shift
    proj = jnp.dot(h.astype(BF16), w_ref[...], preferred_element_type=F32)

    cg = cg_ref[...]
    sg = sg_ref[...]

    def rope_gqa(v):
        return v * cg + pltpu.roll(v, HEAD_DIM // 2, 1) * sg

    q_scale = HEAD_DIM ** -0.5 * LOG2E
    for hh in range(GQA_HEADS):
        q = _rms(proj[:, OFF_GQ + hh * HEAD_DIM: OFF_GQ + (hh + 1) * HEAD_DIM], qg_ref[...])
        qg_o[hh] = (rope_gqa(q) * q_scale).astype(BF16)
    for hh in range(GQA_KV_HEADS):
        k = _rms(proj[:, OFF_GK + hh * HEAD_DIM: OFF_GK + (hh + 1) * HEAD_DIM], kg_ref[...])
        kg_o[hh] = rope_gqa(k).astype(BF16)
        vg_o[hh] = proj[:, OFF_GV + hh * HEAD_DIM: OFF_GV + (hh + 1) * HEAD_DIM].T.astype(BF16)

    lx_o[...] = proj[:, OFF_LX:OFF_LX + LRU_WIDTH]
    lg_o[...] = proj[:, OFF_LG:OFF_LG + LRU_WIDTH]

    cq = _rms(proj[:, OFF_CQ:OFF_CQ + MLA_Q_RANK], mqg_ref[...])
    qm = jnp.dot(cq.astype(BF16), wuq_ref[...], preferred_element_type=F32)
    ckv = _rms(proj[:, OFF_CKV:OFF_CKV + MLA_KV_RANK], mkg_ref[...])
    kv = jnp.dot(ckv.astype(BF16), wukv_ref[...], preferred_element_type=F32)

    cm = cm_ref[...]
    sm = sm_ref[...]

    def rope_mla(v):
        swapped = pltpu.roll(v, MLA_ROPE // 2, 1) + pltpu.roll(v, LANES - MLA_ROPE // 2, 1)
        return v * cm + swapped * sm

    kr = rope_mla(proj[:, OFF_KR:OFF_KR + LANES])
    m_scale = (MLA_NOPE + MLA_ROPE) ** -0.5 * LOG2E
    for hh in range(MLA_HEADS):
        base = hh * MLA_QK_PAD
        qn = qm[:, base:base + MLA_NOPE]
        qr = rope_mla(qm[:, base + MLA_NOPE:base + MLA_QK_PAD])
        qm_o[hh] = (jnp.concatenate([qn, qr], axis=1) * m_scale).astype(BF16)
        kvb = hh * (MLA_NOPE + MLA_V)
        km_o[hh] = jnp.concatenate([kv[:, kvb:kvb + MLA_NOPE], kr], axis=1).astype(BF16)
        vm_o[hh] = kv[:, kvb + MLA_NOPE:kvb + MLA_NOPE + MLA_V].T.astype(BF16)


def _inproj(x, mod, w_in, q_gain, k_gain, mla_q_gain, w_uq, mla_kv_gain, w_ukv, tabs, n_lat):
    t, d = x.shape
    tm = ROW_TILE
    n_lat_tiles = n_lat // tm
    full = lambda a: pl.BlockSpec(a.shape, lambda i: (0,) * a.ndim)
    row = lambda w: pl.BlockSpec((tm, w), lambda i: (i, 0))
    heads = lambda n, w: pl.BlockSpec((n, tm, w), lambda i: (0, i, 0))
    heads_t = lambda n, w: pl.BlockSpec((n, w, tm), lambda i: (0, 0, i))
    cg, sg, cm, sm = tabs
    return pl.pallas_call(
        _inproj_kernel,
        grid=(t // tm,),
        in_specs=[row(d),
                  pl.BlockSpec((1, 6, d), lambda i: (jnp.where(i < n_lat_tiles, 0, 1), 0, 0)),
                  full(w_in), full(q_gain), full(k_gain), full(mla_q_gain), full(w_uq), full(mla_kv_gain),
                  full(w_ukv), row(LANES), row(LANES), row(LANES), row(LANES)],
        out_specs=[heads(GQA_HEADS, HEAD_DIM), heads(GQA_KV_HEADS, HEAD_DIM), heads_t(GQA_KV_HEADS, HEAD_DIM),
                   heads(MLA_HEADS, MLA_QK_PAD), heads(MLA_HEADS, MLA_QK_PAD), heads_t(MLA_HEADS, MLA_V),
                   row(LRU_WIDTH), row(LRU_WIDTH)],
        out_shape=[jax.ShapeDtypeStruct((GQA_HEADS, t, HEAD_DIM), BF16),
                   jax.ShapeDtypeStruct((GQA_KV_HEADS, t, HEAD_DIM), BF16),
                   jax.ShapeDtypeStruct((GQA_KV_HEADS, HEAD_DIM, t), BF16),
                   jax.ShapeDtypeStruct((MLA_HEADS, t, MLA_QK_PAD), BF16),
                   jax.ShapeDtypeStruct((MLA_HEADS, t, MLA_QK_PAD), BF16),
                   jax.ShapeDtypeStruct((MLA_HEADS, MLA_V, t), BF16),
                   jax.ShapeDtypeStruct((t, LRU_WIDTH), F32),
                   jax.ShapeDtypeStruct((t, LRU_WIDTH), F32)],
        compiler_params=_params(("arbitrary",)),
        name="inproj",
    )(x, mod, w_in, q_gain, k_gain, mla_q_gain, w_uq, mla_kv_gain, w_ukv, cg, sg, cm, sm)


ONES_ROWS = 16


def _scores_t(q, k_ref, start, size):
    k = k_ref[0, pl.ds(start, size), :]
    return lax.dot_general(k, q, (((1,), (1,)), ((), ())), preferred_element_type=F32)


def _softmax_step(st, vt_ref, start, size, carry):
    m, acc = carry
    ones = jnp.where(lax.broadcasted_iota(jnp.int32, (ONES_ROWS, size), 0) == 0, 1.0, 0.0).astype(BF16)
    vt = jnp.concatenate([vt_ref[0, :, pl.ds(start, size)], ones], axis=0)
    m_new = jnp.maximum(m, jnp.max(st, axis=0, keepdims=True))
    alpha = jnp.exp2(m - m_new)
    pt = jnp.exp2(st - m_new).astype(BF16)
    acc = alpha * acc + jnp.dot(vt, pt, preferred_element_type=F32)
    return m_new, acc


def _attn_init(tq, dv):
    return jnp.full((1, tq), -1e30, F32), jnp.zeros((dv + ONES_ROWS, tq), F32)


def _attn_finish(carry, o_ref):
    _, acc = carry
    dv = acc.shape[0] - ONES_ROWS
    o_ref[...] = (acc[:dv, :] / acc[dv:dv + 1, :]).T.astype(o_ref.dtype)


def _attn_latent_kernel(n_main, tk, tail, unroll, q_ref, k_ref, vt_ref, o_ref, s_ref):
    hp, tq = q_ref.shape[0], q_ref.shape[1]
    dv = vt_ref.shape[1]
    qs = [q_ref[h] for h in range(hp)]
    step = _softmax_step

    def put(h, slot, st):
        s_ref[h, slot] = st

    def get(h, slot):
        return s_ref[h, slot]

    def body(jj, carries):
        c0 = pl.multiple_of(jj * (2 * tk), 2 * tk)
        carries = list(carries)
        for h in range(hp):
            put(h, 1, _scores_t(qs[h], k_ref, c0 + tk, tk))
            carries[h] = step(get(h, 0), vt_ref, c0, tk, carries[h])
        for h in range(hp):
            put(h, 0, _scores_t(qs[h], k_ref, c0 + 2 * tk, tk))
            carries[h] = step(get(h, 1), vt_ref, c0 + tk, tk, carries[h])
        return tuple(carries)

    carries = tuple(_attn_init(tq, dv) for _ in range(hp))
    for h in range(hp):
        put(h, 0, _scores_t(qs[h], k_ref, 0, tk))
    n_loop = (n_main - 1) // 2
    carries = list(lax.fori_loop(0, n_loop, body, carries, unroll=unroll))
    c0 = 2 * n_loop * tk
    two_left = n_main - 2 * n_loop == 2
    for h in range(hp):
        if two_left:
            put(h, 1, _scores_t(qs[h], k_ref, c0 + tk, tk))
        s_tail = _scores_t(qs[h], k_ref, n_main * tk, tail) if tail else None
        carries[h] = step(get(h, 0), vt_ref, c0, tk, carries[h])
        if two_left:
            carries[h] = step(get(h, 1), vt_ref, c0 + tk, tk, carries[h])
        if tail:
            carries[h] = step(s_tail, vt_ref, n_main * tk, tail, carries[h])
        _attn_finish(carries[h], o_ref.at[:, h * dv:(h + 1) * dv])


def _attn_context_kernel(q_ref, k_ref, vt_ref, o_ref):
    n = k_ref.shape[1]
    carry = _attn_init(q_ref.shape[1], vt_ref.shape[1])
    carry = _softmax_step(_scores_t(q_ref[0], k_ref, 0, n), vt_ref, 0, n, carry)
    _attn_finish(carry, o_ref)


def _attention(q, k, vt, n_lat, group, name, tk, unroll, hp):
    n_heads, t, dk = q.shape
    dv = vt.shape[1]
    n_ctx = t - n_lat
    tq = ATTN_TQ if n_lat % ATTN_TQ == 0 else ROW_TILE
    if t % tk == 0:
        n_main, tail = t // tk, 0
    else:
        tk = tk if n_lat % tk == 0 else ROW_TILE
        n_main, tail = n_lat // tk, n_ctx
    assert n_main * tk + tail == t and group % hp == 0 and n_heads % hp == 0
    lat = pl.pallas_call(
        functools.partial(_attn_latent_kernel, n_main, tk, tail, unroll),
        grid=(n_heads // hp, n_lat // tq),
        in_specs=[pl.BlockSpec((hp, tq, dk), lambda h, i: (h, i, 0)),
                  pl.BlockSpec((1, t, dk), lambda h, i: ((h * hp) // group, 0, 0)),
                  pl.BlockSpec((1, dv, t), lambda h, i: ((h * hp) // group, 0, 0))],
        out_specs=pl.BlockSpec((tq, hp * dv), lambda h, i: (i, h)),
        out_shape=jax.ShapeDtypeStruct((n_lat, n_heads * dv), BF16),
        scratch_shapes=[pltpu.VMEM((hp, 2, tk, tq), F32)],
        compiler_params=_params(("arbitrary", "arbitrary")),
        name=name + "_latent",
    )(q, k, vt)
    cblk = n_lat // n_ctx
    ctx = pl.pallas_call(
        _attn_context_kernel,
        grid=(n_heads,),
        in_specs=[pl.BlockSpec((1, n_ctx, dk), lambda h: (h, cblk, 0)),
                  pl.BlockSpec((1, n_ctx, dk), lambda h: (h // group, cblk, 0)),
                  pl.BlockSpec((1, dv, n_ctx), lambda h: (h // group, 0, cblk))],
        out_specs=pl.BlockSpec((n_ctx, dv), lambda h: (0, h)),
        out_shape=jax.ShapeDtypeStruct((n_ctx, n_heads * dv), BF16),
        compiler_params=_params(("arbitrary",)),
        name=name + "_context",
    )(q, k, vt)
    return lat, ctx


def _conv_kernel(n_lat, t_total, xp_ref, xc_ref, xn_ref, w_ref, b_ref, o_ref):
    i = pl.program_id(0)
    r = xc_ref.shape[0]
    halo = 8
    ext = jnp.concatenate([xp_ref[r - halo:r, :], xc_ref[...], xn_ref[0:halo, :]], axis=0)
    rows = i * r + lax.broadcasted_iota(jnp.int32, (r, 1), 0)
    in_ctx = rows >= n_lat
    seg_lo = jnp.where(in_ctx, n_lat, 0)
    seg_hi = jnp.where(in_ctx, t_total, n_lat)
    left = CONV_WIDTH // 2
    out = jnp.broadcast_to(b_ref[...], (r, xc_ref.shape[1]))
    for tap in range(CONV_WIDTH):
        off = tap - left
        shifted = ext if off == 0 else pltpu.roll(ext, (-off) % (r + 2 * halo), 0)
        xs = shifted[halo:halo + r, :]
        src = rows + off
        valid = jnp.logical_and(src >= seg_lo, src < seg_hi)
        out = out + jnp.where(valid, xs, 0.0) * w_ref[tap:tap + 1, :]
    o_ref[...] = out


def _conv(lx, conv_w, conv_b, n_lat):
    t, w = lx.shape
    r = ROW_TILE
    nt = t // r
    return pl.pallas_call(
        functools.partial(_conv_kernel, n_lat, t),
        grid=(nt,),
        in_specs=[pl.BlockSpec((r, w), lambda i: (jnp.maximum(i - 1, 0), 0)),
                  pl.BlockSpec((r, w), lambda i: (i, 0)),
                  pl.BlockSpec((r, w), lambda i: (jnp.minimum(i + 1, nt - 1), 0)),
                  pl.BlockSpec(conv_w.shape, lambda i: (0, 0)),
                  pl.BlockSpec((1, w), lambda i: (0, 0))],
        out_specs=pl.BlockSpec((r, w), lambda i: (i, 0)),
        out_shape=jax.ShapeDtypeStruct((t, w), F32),
        compiler_params=_params(("arbitrary",)),
        name="lru_conv",
    )(lx, lx, lx, conv_w, conv_b.reshape(1, w))


def _lru_coeffs(x, d, wr_ref, br_ref, wi_ref, bi_ref, lam_ref):
    xb = x.astype(BF16)
    r_parts, i_parts = [], []
    for n in range(LRU_BLOCKS):
        xs = xb[:, n * LRU_BLOCK_W:(n + 1) * LRU_BLOCK_W]
        r_parts.append(jnp.dot(xs, wr_ref[d, n], preferred_element_type=F32))
        i_parts.append(jnp.dot(xs, wi_ref[d, n], preferred_element_type=F32))
    r = jax.nn.sigmoid(jnp.concatenate(r_parts, axis=1) + br_ref[d:d + 1, :])
    ig = jax.nn.sigmoid(jnp.concatenate(i_parts, axis=1) + bi_ref[d:d + 1, :])
    neg_lam = -lam_ref[d:d + 1, :]
    softplus = jnp.maximum(neg_lam, 0.0) + jnp.log1p(jnp.exp(-jnp.abs(neg_lam)))
    log_a = -LRU_C * r * softplus
    a = jnp.exp(log_a)
    u = jnp.sqrt(-jnp.tanh(log_a) * (a * a + 1.0)) * ig * x
    return a, u


def _tile_scan(a, u, reverse):
    r = a.shape[0]
    rows = lax.broadcasted_iota(jnp.int32, (r, 1), 0)
    d = 1
    while d < r:
        if reverse:
            a_sh = pltpu.roll(a, r - d, 0)
            u_sh = pltpu.roll(u, r - d, 0)
            ok = rows < r - d
        else:
            a_sh = pltpu.roll(a, d, 0)
            u_sh = pltpu.roll(u, d, 0)
            ok = rows >= d
        u = jnp.where(ok, a * u_sh + u, u)
        a = jnp.where(ok, a * a_sh, a)
        d *= 2
    return a, u


def _scan_kernel(nc_tiles, xf_ref, xb_ref, wr_ref, br_ref, wi_ref, bi_ref, lam_ref, hf_ref, hb_ref, carry_ref):
    s = pl.program_id(0)
    r = xf_ref.shape[0]

    @pl.when(s == 0)
    def _():
        carry_ref[...] = jnp.zeros_like(carry_ref)

    del nc_tiles
    a, u = _lru_coeffs(xf_ref[...], 0, wr_ref, br_ref, wi_ref, bi_ref, lam_ref)
    a, u = _tile_scan(a, u, False)
    h = a * carry_ref[0:1, :] + u
    hf_ref[...] = h
    carry_ref[0:1, :] = h[r - 1:r, :]

    a, u = _lru_coeffs(xb_ref[...], 1, wr_ref, br_ref, wi_ref, bi_ref, lam_ref)
    a, u = _tile_scan(a, u, True)
    h = a * carry_ref[1:2, :] + u
    hb_ref[...] = h
    carry_ref[1:2, :] = h[0:1, :]


def _lru_scan(xc, w_r, b_r, w_i, b_i, lam, n_lat):
    t, w = xc.shape
    r = ROW_TILE
    nt, nl = t // r, n_lat // r
    nc = nt - nl
    fwd = lambda s: (jnp.where(s < nc, nl + s, s - nc), 0)
    bwd = lambda s: (jnp.where(s < nc, nt - 1 - s, nl - 1 - (s - nc)), 0)
    full = lambda a: pl.BlockSpec(a.shape, lambda s: (0,) * a.ndim)
    return pl.pallas_call(
        functools.partial(_scan_kernel, nc),
        grid=(nt,),
        in_specs=[pl.BlockSpec((r, w), fwd), pl.BlockSpec((r, w), bwd),
                  full(w_r), full(b_r), full(w_i), full(b_i), full(lam)],
        out_specs=[pl.BlockSpec((r, w), fwd), pl.BlockSpec((r, w), bwd)],
        out_shape=[jax.ShapeDtypeStruct((t, w), F32), jax.ShapeDtypeStruct((t, w), F32)],
        scratch_shapes=[pltpu.VMEM((8, w), F32)],
        compiler_params=_params(("arbitrary",)),
        name="lru_scan",
    )(xc, xc, w_r, b_r, w_i, b_i, lam)


def _layer_norm(y, g, b):
    mu = jnp.mean(y, axis=-1, keepdims=True)
    yc = y - mu
    var = jnp.mean(yc * yc, axis=-1, keepdims=True)
    return yc * lax.rsqrt(var + EPS) * g + b


def _gelu_tanh(x):
    return 0.5 * x * (1.0 + jnp.tanh(math.sqrt(2.0 / math.pi) * (x + 0.044715 * (x * x * x))))


def _outproj_kernel(alpha, with_router, n_lat_tiles, al_ref, ac_ref, ml_ref, mc_ref, hf_ref, hb_ref, lg_ref, x_ref,
                    w_ref, mod_ref, g_ref, b_ref, *rest):
    if with_router:
        router_ref, x1_o, fin_o, topi_o, gate_o = rest
    else:
        x1_o, fin_o = rest
    is_lat = pl.program_id(0) < n_lat_tiles
    a = jnp.where(is_lat, al_ref[...], ac_ref[...])
    m = jnp.where(is_lat, ml_ref[...], mc_ref[...])
    rec = _gelu_tanh(lg_ref[...]) * (hf_ref[...] + hb_ref[...])
    mix = jnp.dot(a, w_ref[0:D_GQA, :], preferred_element_type=F32)
    mix += jnp.dot(rec.astype(BF16), w_ref[D_GQA:D_GQA + LRU_WIDTH, :], preferred_element_type=F32)
    mix += jnp.dot(m, w_ref[D_GQA + LRU_WIDTH:, :], preferred_element_type=F32)
    y = alpha * x_ref[...] + mod_ref[0, 2:3, :] * mix
    x1 = _layer_norm(y, g_ref[...], b_ref[...])
    x1_o[...] = x1
    f_in = x1 * (1.0 + mod_ref[0, 4:5, :]) + mod_ref[0, 3:4, :]
    fin_o[...] = f_in
    if with_router:
        f_hi = f_in.astype(BF16)
        f_lo = (f_in - f_hi.astype(F32)).astype(BF16)
        r = router_ref[...]
        r_hi = r.astype(BF16)
        r_lo = (r - r_hi.astype(F32)).astype(BF16)
        logits = (jnp.dot(f_hi, r_hi, preferred_element_type=F32) + jnp.dot(f_hi, r_lo, preferred_element_type=F32)
                  + jnp.dot(f_lo, r_hi, preferred_element_type=F32))
        lane = lax.broadcasted_iota(jnp.int32, logits.shape, 1).astype(F32)
        neg = jnp.float32(-jnp.inf)
        lgt = jnp.where(lane < N_EXPERTS, logits, neg)
        v1 = jnp.max(lgt, axis=-1, keepdims=True)
        i1 = jnp.min(jnp.where(lgt == v1, lane, float(LANES)), axis=-1, keepdims=True)
        lgt2 = jnp.where(lane == i1, neg, lgt)
        v2 = jnp.max(lgt2, axis=-1, keepdims=True)
        i2 = jnp.min(jnp.where(lgt2 == v2, lane, float(LANES)), axis=-1, keepdims=True)
        e2 = jnp.exp(v2 - v1)
        den = 1.0 + e2
        topi_o[...] = jnp.where(lane == 0.0, i1, jnp.where(lane == 1.0, i2, 0.0)).astype(jnp.int32)
        gate_o[...] = jnp.where(lane == 0.0, 1.0 / den, jnp.where(lane == 1.0, e2 / den, 0.0))


def _outproj(a, m, hf, hb, lg, x, w_out, mod, ln_g, ln_b, alpha, n_lat, router):
    t, d = x.shape
    tm = ROW_TILE
    n_lat_tiles = n_lat // tm
    row = lambda w: pl.BlockSpec((tm, w), lambda i: (i, 0))
    lat_row = lambda w: pl.BlockSpec((tm, w), lambda i: (jnp.minimum(i, n_lat_tiles - 1), 0))
    ctx_row = lambda w: pl.BlockSpec((tm, w), lambda i: (jnp.maximum(i - n_lat_tiles, 0), 0))
    full = lambda arr: pl.BlockSpec(arr.shape, lambda i: (0,) * arr.ndim)
    in_specs = [lat_row(D_GQA), ctx_row(D_GQA), lat_row(D_MLA), ctx_row(D_MLA),
                row(LRU_WIDTH), row(LRU_WIDTH), row(LRU_WIDTH), row(d),
                full(w_out), pl.BlockSpec((1, 6, d), lambda i: (jnp.where(i < n_lat_tiles, 0, 1), 0, 0)),
                full(ln_g), full(ln_b)]
    out_specs = [row(d), row(d)]
    out_shape = [jax.ShapeDtypeStruct((t, d), F32), jax.ShapeDtypeStruct((t, d), F32)]
    args = [a[0], a[1], m[0], m[1], hf, hb, lg, x, w_out, mod, ln_g, ln_b]
    if router is not None:
        in_specs.append(full(router))
        args.append(router)
        out_specs += [row(LANES), row(LANES)]
        out_shape += [jax.ShapeDtypeStruct((t, LANES), jnp.int32), jax.ShapeDtypeStruct((t, LANES), F32)]
    return pl.pallas_call(
        functools.partial(_outproj_kernel, alpha, router is not None, n_lat_tiles),
        grid=(t // tm,),
        in_specs=in_specs, out_specs=out_specs, out_shape=out_shape,
        compiler_params=_params(("arbitrary",)),
        name="outproj",
    )(*args)


def _ffn_kernel(te_ref, nv_ref, x_ref, wg_ref, wu_ref, wd_ref, sc_ref, o_ref, acc_ref):
    del te_ref
    i, j = pl.program_id(0), pl.program_id(1)

    @pl.when(j == 0)
    def _():
        acc_ref[...] = jnp.zeros_like(acc_ref)

    @pl.when(i < nv_ref[0])
    def _():
        xb = x_ref[...].astype(BF16)
        g = jnp.dot(xb, wg_ref[0], preferred_element_type=F32)
        u = jnp.dot(xb, wu_ref[0], preferred_element_type=F32)
        hidden = (g * jax.nn.sigmoid(g)) * u
        acc_ref[...] += jnp.dot(hidden.astype(BF16), wd_ref[0], preferred_element_type=F32)

    @pl.when(j == pl.num_programs(1) - 1)
    def _():
        o_ref[...] = acc_ref[...] * sc_ref[...]


def _ffn(x, tile_expert, n_valid, w_gate, w_up, w_down, row_scale, tm):
    m_rows, d = x.shape
    f = w_gate.shape[2]
    tf = next((c for c in FFN_TF_CHOICES if f % c == 0), f)
    nj = f // tf
    col = lambda i, j, nv: jnp.where(i < nv[0], j, nj - 1)
    grid_spec = pltpu.PrefetchScalarGridSpec(
        num_scalar_prefetch=2,
        grid=(m_rows // tm, nj),
        in_specs=[pl.BlockSpec((tm, d), lambda i, j, te, nv: (i, 0)),
                  pl.BlockSpec((1, d, tf), lambda i, j, te, nv: (te[i], 0, col(i, j, nv))),
                  pl.BlockSpec((1, d, tf), lambda i, j, te, nv: (te[i], 0, col(i, j, nv))),
                  pl.BlockSpec((1, tf, d), lambda i, j, te, nv: (te[i], col(i, j, nv), 0)),
                  pl.BlockSpec((tm, 1), lambda i, j, te, nv: (i, 0))],
        out_specs=pl.BlockSpec((tm, d), lambda i, j, te, nv: (i, 0)),
        scratch_shapes=[pltpu.VMEM((tm, d), F32)],
    )
    return pl.pallas_call(
        _ffn_kernel,
        grid_spec=grid_spec,
        out_shape=jax.ShapeDtypeStruct((m_rows, d), F32),
        compiler_params=_params(("arbitrary", "arbitrary")),
        name="ffn",
    )(tile_expert, n_valid, x, w_gate, w_up, w_down, row_scale)


def _ffn_ln_kernel(alpha, n_lat, te_ref, x_ref, wg_ref, wu_ref, wd_ref, x1_ref, mod_ref, g_ref, b_ref, o_ref, acc_ref):
    del te_ref
    i, j = pl.program_id(0), pl.program_id(1)

    @pl.when(j == 0)
    def _():
        acc_ref[...] = jnp.zeros_like(acc_ref)

    xb = x_ref[...].astype(BF16)
    g = jnp.dot(xb, wg_ref[0], preferred_element_type=F32)
    u = jnp.dot(xb, wu_ref[0], preferred_element_type=F32)
    hidden = (g * jax.nn.sigmoid(g)) * u
    acc_ref[...] += jnp.dot(hidden.astype(BF16), wd_ref[0], preferred_element_type=F32)

    @pl.when(j == pl.num_programs(1) - 1)
    def _():
        tm = acc_ref.shape[0]
        rows = i * tm + lax.broadcasted_iota(jnp.int32, (tm, 1), 0)
        gate = jnp.where(rows < n_lat, mod_ref[0, 5:6, :], mod_ref[1, 5:6, :])
        y = alpha * x1_ref[...] + gate * acc_ref[...]
        o_ref[...] = _layer_norm(y, g_ref[...], b_ref[...])


def _ffn_ln(x, layer, w_gate, w_up, w_down, x1, mod, ln_g, ln_b, alpha, n_lat, tm):
    t, d = x.shape
    f = w_gate.shape[2]
    tf = next((c for c in FFN_TF_CHOICES if f % c == 0), f)
    full = lambda arr: pl.BlockSpec(arr.shape, lambda i, j, te: (0,) * arr.ndim)
    grid_spec = pltpu.PrefetchScalarGridSpec(
        num_scalar_prefetch=1,
        grid=(t // tm, f // tf),
        in_specs=[pl.BlockSpec((tm, d), lambda i, j, te: (i, 0)),
                  pl.BlockSpec((1, d, tf), lambda i, j, te: (te[0], 0, j)),
                  pl.BlockSpec((1, d, tf), lambda i, j, te: (te[0], 0, j)),
                  pl.BlockSpec((1, tf, d), lambda i, j, te: (te[0], j, 0)),
                  pl.BlockSpec((tm, d), lambda i, j, te: (i, 0)),
                  full(mod), full(ln_g), full(ln_b)],
        out_specs=pl.BlockSpec((tm, d), lambda i, j, te: (i, 0)),
        scratch_shapes=[pltpu.VMEM((tm, d), F32)],
    )
    return pl.pallas_call(
        functools.partial(_ffn_ln_kernel, alpha, n_lat),
        grid_spec=grid_spec,
        out_shape=jax.ShapeDtypeStruct((t, d), F32),
        compiler_params=_params(("arbitrary", "arbitrary")),
        name="ffn_ln",
    )(jnp.full((1,), layer, jnp.int32), x, w_gate, w_up, w_down, x1, mod, ln_g, ln_b)


def _gather_kernel(idx_ref, src_ref, dst_ref, sem):
    base = pl.program_id(0) * GATHER_CHUNK

    def issue(g, carry):
        for k in range(GATHER_UNROLL):
            r = g * GATHER_UNROLL + k
            pltpu.make_async_copy(src_ref.at[pl.ds(idx_ref[base + r], 1)], dst_ref.at[pl.ds(r, 1)], sem).start(
                priority=k % 2)
        return carry

    lax.fori_loop(0, GATHER_CHUNK // GATHER_UNROLL, issue, 0)
    pltpu.make_async_copy(src_ref.at[pl.ds(0, GATHER_CHUNK)], dst_ref, sem).wait()


def _row_gather(src, idx):
    m_rows = idx.shape[0]
    assert m_rows % GATHER_CHUNK == 0
    grid_spec = pltpu.PrefetchScalarGridSpec(
        num_scalar_prefetch=1,
        grid=(m_rows // GATHER_CHUNK,),
        in_specs=[pl.BlockSpec(memory_space=pl.ANY)],
        out_specs=pl.BlockSpec((GATHER_CHUNK, src.shape[1]), lambda c, idx: (c, 0)),
        scratch_shapes=[pltpu.SemaphoreType.DMA(())],
    )
    return pl.pallas_call(
        _gather_kernel,
        grid_spec=grid_spec,
        out_shape=jax.ShapeDtypeStruct((m_rows, src.shape[1]), src.dtype),
        compiler_params=_params(("arbitrary",)),
        name="row_gather",
    )(idx, src)


def _ln2_kernel(alpha, n_parts, x_ref, f_ref, mod_ref, g_ref, b_ref, o_ref):
    f = f_ref[0]
    for p in range(1, n_parts):
        f = f + f_ref[p]
    y = alpha * x_ref[...] + mod_ref[0, 5:6, :] * f
    o_ref[...] = _layer_norm(y, g_ref[...], b_ref[...])


def _ln2(x1, f_parts, mod, ln_g, ln_b, alpha, n_lat, n_rows):
    n_parts, _, d = f_parts.shape
    tm = ROW_TILE
    n_lat_tiles = n_lat // tm
    full = lambda arr: pl.BlockSpec(arr.shape, lambda i: (0,) * arr.ndim)
    return pl.pallas_call(
        functools.partial(_ln2_kernel, alpha, n_parts),
        grid=(n_rows // tm,),
        in_specs=[pl.BlockSpec((tm, d), lambda i: (i, 0)),
                  pl.BlockSpec((n_parts, tm, d), lambda i: (0, i, 0)),
                  pl.BlockSpec((1, 6, d), lambda i: (jnp.where(i < n_lat_tiles, 0, 1), 0, 0)),
                  full(ln_g), full(ln_b)],
        out_specs=pl.BlockSpec((tm, d), lambda i: (i, 0)),
        out_shape=jax.ShapeDtypeStruct((n_rows, d), F32),
        compiler_params=_params(("arbitrary",)),
        name="ln2",
    )(x1, f_parts, mod, ln_g, ln_b)


def _rope_tables(n_lat, n_ctx):
    rows = n_lat // GRID_W

    def angles(rot_dim):
        quarter = rot_dim // 4
        inv_freq = ROPE_THETA ** (-jnp.arange(quarter, dtype=F32) / quarter)
        row = jnp.repeat(jnp.arange(rows, dtype=F32), GRID_W)
        col = jnp.tile(jnp.arange(GRID_W, dtype=F32), rows)
        ang = jnp.concatenate([row[:, None] * inv_freq, col[:, None] * inv_freq], axis=-1)
        return jnp.cos(ang), jnp.sin(ang)

    def with_ctx(tab, fill):
        return jnp.concatenate([tab, jnp.full((n_ctx, tab.shape[1]), fill, F32)], axis=0)

    cos_g, sin_g = angles(HEAD_DIM)
    cg = with_ctx(jnp.concatenate([cos_g, cos_g], axis=1), 1.0)
    sg = with_ctx(jnp.concatenate([-sin_g, sin_g], axis=1), 0.0)
    cos_m, sin_m = angles(MLA_ROPE)
    zeros = jnp.zeros((n_lat, LANES - MLA_ROPE), F32)
    cm = with_ctx(jnp.concatenate([cos_m, cos_m, zeros], axis=1), 1.0)
    sm = with_ctx(jnp.concatenate([-sin_m, sin_m, zeros], axis=1), 0.0)
    return cg, sg, cm, sm


def _pad_w_uq(w_uq):
    depth, rank, _ = w_uq.shape
    w = w_uq.reshape(depth, rank, MLA_HEADS, MLA_NOPE + MLA_ROPE)
    w = jnp.pad(w, ((0, 0), (0, 0), (0, 0), (0, MLA_QK_PAD - MLA_NOPE - MLA_ROPE)))
    return w.reshape(depth, rank, MLA_HEADS * MLA_QK_PAD)


def _dispatch(top_i, gates, tm):
    t = top_i.shape[0]
    flat_e = top_i.T.reshape(-1)
    flat_g = gates.T.reshape(-1)
    onehot = (flat_e[:, None] == jnp.arange(N_EXPERTS, dtype=jnp.int32)[None, :]).astype(jnp.int32)
    rank = jnp.sum((jnp.cumsum(onehot, axis=0) - onehot) * onehot, axis=1)
    counts = jnp.sum(onehot, axis=0)
    ptiles = (counts + tm - 1) // tm
    tile_end = jnp.cumsum(ptiles)
    tile_start = tile_end - ptiles
    pos = (tile_start[flat_e] * tm + rank).astype(jnp.int32)
    n_tiles = (2 * t) // tm + N_EXPERTS
    m_pad = n_tiles * tm
    src = (jnp.arange(m_pad, dtype=jnp.int32) % t).at[pos].set(jnp.tile(jnp.arange(t, dtype=jnp.int32), 2))
    scale = jnp.zeros((m_pad,), F32).at[pos].set(flat_g)
    tile_ids = jnp.arange(n_tiles, dtype=jnp.int32)
    last_used = jnp.max(jnp.where(ptiles > 0, jnp.arange(N_EXPERTS, dtype=jnp.int32), 0))
    tile_expert = jnp.minimum(jnp.sum((tile_end[None, :] <= tile_ids[:, None]).astype(jnp.int32), axis=1),
                              last_used).astype(jnp.int32)
    n_valid = tile_end[-1:].astype(jnp.int32)
    return src, scale.reshape(m_pad, 1), pos, tile_expert, n_valid


def kernel(x, c, ctx, c_ctx, ada_w, ada_b, ln_g, ln_b, w_in, w_out, gqa_q_gain, gqa_k_gain, lru_conv_w, lru_conv_b, lru_w_r, lru_b_r, lru_w_i, lru_b_i, lru_lambda, mla_q_gain, mla_w_uq, mla_kv_gain, mla_w_ukv, ffn_w_gate, ffn_w_up, ffn_w_down, moe_router, moe_w_gate, moe_w_up, moe_w_down):
    batch, n_lat, d = x.shape
    n_ctx = ctx.shape[1]
    depth = ada_w.shape[0]
    assert batch == 1 and ctx.shape[0] == 1
    assert n_lat % ROW_TILE == 0 and n_ctx % ROW_TILE == 0 and n_lat % n_ctx == 0 and n_lat % GRID_W == 0
    t = n_lat + n_ctx
    alpha = (2 * depth) ** 0.25

    xs = jnp.concatenate([x[0], ctx[0]], axis=0)
    cond_t = jnp.stack([c[0], c_ctx], axis=1)
    mods = _modulation(cond_t, ada_w, ada_b).reshape(depth, 2, 6, d)
    tabs = _rope_tables(n_lat, n_ctx)

    w_in_b = jnp.pad(w_in, ((0, 0), (0, 0), (0, D_IN_PAD - D_IN))).astype(BF16)
    w_out_b = w_out.astype(BF16)
    w_uq_b = _pad_w_uq(mla_w_uq).astype(BF16)
    w_ukv_b = mla_w_ukv.astype(BF16)
    w_r_b = lru_w_r.astype(BF16)
    w_i_b = lru_w_i.astype(BF16)
    router_p = jnp.pad(moe_router, ((0, 0), (0, 0), (0, LANES - N_EXPERTS)))
    ffn_tm = FFN_TM if t % FFN_TM == 0 else ROW_TILE
    moe_tm = MOE_TM if (2 * t) % MOE_TM == 0 else ROW_TILE
    n_moe, d_exp = moe_w_gate.shape[0], moe_w_gate.shape[3]
    dense_w = (ffn_w_gate.astype(BF16), ffn_w_up.astype(BF16), ffn_w_down.astype(BF16))
    moe_w = (moe_w_gate.astype(BF16).reshape(n_moe * N_EXPERTS, d, d_exp),
             moe_w_up.astype(BF16).reshape(n_moe * N_EXPERTS, d, d_exp),
             moe_w_down.astype(BF16).reshape(n_moe * N_EXPERTS, d_exp, d))
    gqa_tk = next((c for c in GQA_TK_CHOICES if t % c == 0), ATTN_TK)
    mla_tk = next((c for c in MLA_TK_CHOICES if t % c == 0), ATTN_TK)

    for l in range(depth):
        last = l == depth - 1
        mod = mods[l]
        qg, kg, vg, qm, km, vm, lx, lg = _inproj(
            xs, mod, w_in_b[l], gqa_q_gain[l][None], gqa_k_gain[l][None], mla_q_gain[l][None], w_uq_b[l],
            mla_kv_gain[l][None], w_ukv_b[l], tabs, n_lat)
        a_out = _attention(qg, kg, vg, n_lat, GQA_HEADS // GQA_KV_HEADS, "gqa", tk=gqa_tk, unroll=6, hp=2)
        m_out = _attention(qm, km, vm, n_lat, 1, "mla", tk=mla_tk, unroll=2, hp=1)
        xc = _conv(lx, lru_conv_w[l], lru_conv_b[l], n_lat)
        hf, hb = _lru_scan(xc, w_r_b[l], lru_b_r[l], w_i_b[l], lru_b_i[l], lru_lambda[l], n_lat)
        moe = l % 2 == 1
        outs = _outproj(a_out, m_out, hf, hb, lg, xs, w_out_b[l], mod, ln_g[l, 0][None], ln_b[l, 0][None],
                        alpha, n_lat, router_p[l // 2] if moe else None)
        if moe:
            x1, f_in, topi, gate = outs
            src, scale, pos, tile_expert, n_valid = _dispatch(topi[:, :2], gate[:, :2], moe_tm)
            x_sorted = _row_gather(f_in, src)
            y = _ffn(x_sorted, tile_expert + (l // 2) * N_EXPERTS, n_valid, *moe_w, scale, moe_tm)
            f_parts = _row_gather(y, pos).reshape(2, t, d)
            xs = _ln2(x1, f_parts, mod, ln_g[l, 1][None], ln_b[l, 1][None], alpha, n_lat, n_lat if last else t)
        else:
            x1, f_in = outs
            xs = _ffn_ln(f_in, l // 2, *dense_w, x1, mod, ln_g[l, 1][None], ln_b[l, 1][None], alpha, n_lat, ffn_tm)
            if last:
                xs = xs[:n_lat]
    return xs[None]
```
